```python
import math
import jax, jax.numpy as jnp
from jax import lax
import numpy as np

D_MODEL = 2048
BATCH = 4
SEQ = 4096
DEPTH = 4

GRID_W = 64
CTX_LEN = 256
RMS_EPS = 1e-6
NEG_INF = -1e30

BRANCH_WIDTH = D_MODEL // 4
N_BRANCHES = 4
NA_HEAD_DIM = 64
NA_HEADS = BRANCH_WIDTH // NA_HEAD_DIM
NA_WIDTH = NA_HEADS * NA_HEAD_DIM
NA_WIN_ROWS = 8
NA_WIN_COLS = 16
POOL_WIDTH = BRANCH_WIDTH
POOL_WINDOWS = (2, 4, 8, 16)
POOL_GROUPS = 4
POOL_GROUP_DIM = POOL_WIDTH // POOL_GROUPS
CONV_WIDTH = BRANCH_WIDTH
CONV_K = 3
SSM_WIDTH = BRANCH_WIDTH
SSM_GROUP_DIM = 16
SSM_GROUPS = SSM_WIDTH // SSM_GROUP_DIM
SSM_STATE = 64
SSM_DT_MIN = 1e-3
SSM_DT_MAX = 1e-1

BRANCH_TOTAL = NA_WIDTH + POOL_WIDTH + CONV_WIDTH + SSM_WIDTH
IN_LAYOUT = (
    ("na_q", NA_WIDTH), ("na_k", NA_WIDTH), ("na_v", NA_WIDTH), ("na_z", NA_WIDTH),
    ("pool_u", POOL_WIDTH), ("pool_z", POOL_WIDTH),
    ("conv_x", CONV_WIDTH), ("conv_b", CONV_WIDTH), ("conv_c", CONV_WIDTH), ("conv_z", CONV_WIDTH),
    ("ssm_u", SSM_WIDTH), ("ssm_z", SSM_WIDTH),
    ("merge", N_BRANCHES * D_MODEL),
)
IN_TOTAL = sum(size for _, size in IN_LAYOUT)

kernel_name = "hybrid_gated_mixer_dit_block"


def _in_slices():
    out, start = {}, 0
    for name, size in IN_LAYOUT:
        out[name] = (start, start + size)
        start += size
    return out


def rms_norm(x, g):
    xf = x.astype(jnp.float32)
    y = xf * lax.rsqrt(jnp.mean(xf * xf, axis=-1, keepdims=True) + RMS_EPS)
    return (y * g.astype(jnp.float32)).astype(x.dtype)


def neighbourhood_attention(q, k, v, qc, kc, vc, rpb):
    B, L, H, Dh = q.shape
    rows = L // GRID_W
    kr = min(NA_WIN_ROWS, rows)
    kcw = NA_WIN_COLS
    scale = Dh ** -0.5
    r = jnp.arange(rows)
    row_idx = jnp.clip(r - kr // 2, 0, rows - kr)[:, None] + jnp.arange(kr)[None, :]
    col = jnp.arange(GRID_W)
    col_start = jnp.clip(col - kcw // 2, 0, GRID_W - kcw)
    in_win = (col[None, :] >= col_start[:, None]) & (col[None, :] < col_start[:, None] + kcw)
    drow = row_idx - r[:, None] + (NA_WIN_ROWS - 1)
    dcol = jnp.clip(col[None, :] - col[:, None] + (NA_WIN_COLS - 1), 0, 2 * NA_WIN_COLS - 2)
    bias = rpb[:, drow[:, None, :, None], dcol[None, :, None, :]].astype(jnp.float32)
    bias = jnp.where(in_win[None, None, :, None, :], bias, NEG_INF)

    qg = q.reshape(B, rows, GRID_W, H, Dh)
    kg = k.reshape(B, rows, GRID_W, H, Dh)[:, row_idx]
    vg = v.reshape(B, rows, GRID_W, H, Dh)[:, row_idx]
    s_band = jnp.einsum('brqhd,brkwhd->bhrqkw', qg, kg,
                        preferred_element_type=jnp.float32) * scale + bias[None]
    s_ctx = jnp.einsum('brqhd,bnhd->bhrqn', qg, kc, preferred_element_type=jnp.float32) * scale
    n_band = kr * GRID_W
    s = jnp.concatenate([s_band.reshape(B, H, rows, GRID_W, n_band), s_ctx], axis=-1)
    p = jax.nn.softmax(s, axis=-1)
    p_band = p[..., :n_band].reshape(B, H, rows, GRID_W, kr, GRID_W).astype(v.dtype)
    p_ctx = p[..., n_band:].astype(v.dtype)
    o = (jnp.einsum('bhrqkw,brkwhd->brqhd', p_band, vg)
         + jnp.einsum('bhrqn,bnhd->brqhd', p_ctx, vc))
    o = o.reshape(B, L, H * Dh)
    oc = None
    if qc is not None:
        sc = jnp.einsum('bnhd,bmhd->bhnm', qc, kc, preferred_element_type=jnp.float32) * scale
        pc = jax.nn.softmax(sc, axis=-1).astype(vc.dtype)
        oc = jnp.einsum('bhnm,bmhd->bnhd', pc, vc).reshape(qc.shape[0], qc.shape[1], H * Dh)
    return o, oc


def centred_pool_minus_identity(x, window):
    B, L, C = x.shape
    xf = x.astype(jnp.float32)
    cs = jnp.concatenate([jnp.zeros((B, 1, C), jnp.float32), jnp.cumsum(xf, axis=1)], axis=1)
    t = jnp.arange(L)
    lo = jnp.clip(t - window // 2, 0, L)
    hi = jnp.clip(t + window - window // 2, 0, L)
    cnt = (hi - lo).astype(jnp.float32)[None, :, None]
    return ((cs[:, hi] - cs[:, lo]) / cnt - xf).astype(x.dtype)


def pool_branch(u, pool_w, pool_scale):
    B, L, _ = u.shape
    groups = jnp.split(u, POOL_GROUPS, axis=-1)
    pooled = jnp.stack([centred_pool_minus_identity(g, w) for g, w in zip(groups, POOL_WINDOWS)], axis=2)
    mixed = jnp.einsum('blgc,gcd->blgd', pooled, pool_w).reshape(B, L, POOL_WIDTH)
    return mixed * pool_scale


def dwconv3(x, w):
    L = x.shape[1]
    xp = jnp.pad(x, ((0, 0), (1, 1), (0, 0)))
    return xp[:, :L] * w[0] + xp[:, 1:L + 1] * w[1] + xp[:, 2:] * w[2]


def conv_branch(xv, gb, gc, conv_w):
    return gb * dwconv3(gc * xv, conv_w)


def _cmul(ar, ai, br, bi):
    return ar * br - ai * bi, ar * bi + ai * br


def s5_discretise(a_re, a_im, log_dt, b_re, b_im):
    f32 = jnp.float32
    a_re, a_im = a_re.astype(f32), a_im.astype(f32)
    dt = jnp.exp(log_dt.astype(f32))[:, None]
    mag = jnp.exp(a_re * dt)
    abar_re, abar_im = mag * jnp.cos(a_im * dt), mag * jnp.sin(a_im * dt)
    den = a_re * a_re + a_im * a_im
    num_re, num_im = abar_re - 1.0, abar_im
    f_re = (num_re * a_re + num_im * a_im) / den
    f_im = (num_im * a_re - num_re * a_im) / den
    bbar_re, bbar_im = _cmul(f_re[..., None], f_im[..., None], b_re.astype(f32), b_im.astype(f32))
    return abar_re, abar_im, bbar_re, bbar_im


def diag_scan(abar_re, abar_im, bu_re, bu_im, s0_re, s0_im, reverse):
    L = bu_re.shape[1]
    a_re = jnp.broadcast_to(abar_re, (1, L) + abar_re.shape)
    a_im = jnp.broadcast_to(abar_im, (1, L) + abar_im.shape)

    def combine(e1, e2):
        a1r, a1i, b1r, b1i = e1
        a2r, a2i, b2r, b2i = e2
        ar, ai = _cmul(a2r, a2i, a1r, a1i)
        br, bi = _cmul(a2r, a2i, b1r, b1i)
        return ar, ai, br + b2r, bi + b2i

    pr, pi, sr, si = lax.associative_scan(combine, (a_re, a_im, bu_re, bu_im), axis=1, reverse=reverse)
    if s0_re is not None:
        ir, ii = _cmul(pr, pi, s0_re[:, None], s0_im[:, None])
        sr, si = sr + ir, si + ii
    return sr, si


def s5_readout(s_re, s_im, c_re, c_im):
    B, L = s_re.shape[:2]
    y = (jnp.einsum('blgp,ghp->blgh', s_re, c_re.astype(jnp.float32))
         - jnp.einsum('blgp,ghp->blgh', s_im, c_im.astype(jnp.float32)))
    return y.reshape(B, L, SSM_WIDTH)


def s5_glu(y, glu_w, dtype):
    g = jax.nn.gelu(y).astype(dtype)
    ga, gb = jnp.split(g @ glu_w, 2, axis=-1)
    return ga * jax.nn.sigmoid(gb)


def s5_branch(u, uc, a_re, a_im, log_dt, b_re, b_im, c_re, c_im, d_skip, glu_w, with_ctx_out):
    dtype = u.dtype
    B, L, _ = u.shape
    N = uc.shape[1]
    uf, ucf = u.astype(jnp.float32), uc.astype(jnp.float32)
    ug = uf.reshape(B, L, SSM_GROUPS, SSM_GROUP_DIM)
    ucg = ucf.reshape(B, N, SSM_GROUPS, SSM_GROUP_DIM)
    dsk = d_skip.astype(jnp.float32)
    y = dsk * uf
    yc = dsk * ucf if with_ctx_out else None
    for direction, reverse in ((0, False), (1, True)):
        abr, abi, bbr, bbi = s5_discretise(a_re[direction], a_im[direction], log_dt[direction],
                                           b_re[direction], b_im[direction])
        buc_r = jnp.einsum('blgh,gph->blgp', ucg, bbr)
        buc_i = jnp.einsum('blgh,gph->blgp', ucg, bbi)
        sc_r, sc_i = diag_scan(abr, abi, buc_r, buc_i, None, None, reverse)
        last = 0 if reverse else N - 1
        bu_r = jnp.einsum('blgh,gph->blgp', ug, bbr)
        bu_i = jnp.einsum('blgh,gph->blgp', ug, bbi)
        s_r, s_i = diag_scan(abr, abi, bu_r, bu_i, sc_r[:, last], sc_i[:, last], reverse)
        y = y + s5_readout(s_r, s_i, c_re[direction], c_im[direction])
        if with_ctx_out:
            yc = yc + s5_readout(sc_r, sc_i, c_re[direction], c_im[direction])
    out = s5_glu(y, glu_w, dtype)
    outc = s5_glu(yc, glu_w, dtype) if with_ctx_out else None
    return out, outc


def gated_merge(outs, zs, gate_logits, b_gate, w_br, w_o):
    gates = jax.nn.sigmoid((gate_logits + b_gate).astype(jnp.float32)).astype(gate_logits.dtype)
    merged = None
    start = 0
    for i, (o, z) in enumerate(zip(outs, zs)):
        width = o.shape[-1]
        br = (o * jax.nn.silu(z)) @ w_br[start:start + width]
        term = gates[..., i * D_MODEL:(i + 1) * D_MODEL] * br
        merged = term if merged is None else merged + term
        start += width
    return merged @ w_o


def hybrid_mixer(h, hc, w_in, b_gate, na_rpb, pool_w, pool_scale, conv_w,
                 ssm_a_re, ssm_a_im, ssm_log_dt, ssm_b_re, ssm_b_im, ssm_c_re, ssm_c_im, ssm_d,
                 glu_w, w_br, w_o, with_ctx_out):
    sl = _in_slices()
    B, L, _ = h.shape
    N = hc.shape[1]
    proj = h @ w_in

    def part(name):
        a, b = sl[name]
        return proj[..., a:b]

    if with_ctx_out:
        projc = hc @ w_in

        def partc(name):
            a, b = sl[name]
            return projc[..., a:b]
    else:
        def partc(name):
            a, b = sl[name]
            return hc @ w_in[:, a:b]

    heads = lambda t, n: t.reshape(B if t.shape[1] == L else t.shape[0], n, NA_HEADS, NA_HEAD_DIM)
    q, k, v = heads(part("na_q"), L), heads(part("na_k"), L), heads(part("na_v"), L)
    kc, vc = heads(partc("na_k"), N), heads(partc("na_v"), N)
    qc = heads(partc("na_q"), N) if with_ctx_out else None
    o_na, oc_na = neighbourhood_attention(q, k, v, qc, kc, vc, na_rpb)

    o_pool = pool_branch(part("pool_u"), pool_w, pool_scale)
    o_conv = conv_branch(part("conv_x"), part("conv_b"), part("conv_c"), conv_w)
    o_ssm, oc_ssm = s5_branch(part("ssm_u"), partc("ssm_u"), ssm_a_re, ssm_a_im, ssm_log_dt,
                              ssm_b_re, ssm_b_im, ssm_c_re, ssm_c_im, ssm_d, glu_w, with_ctx_out)
    y = gated_merge([o_na, o_pool, o_conv, o_ssm],
                    [part("na_z"), part("pool_z"), part("conv_z"), part("ssm_z")],
                    part("merge"), b_gate, w_br, w_o)
    yc = None
    if with_ctx_out:
        oc_pool = pool_branch(partc("pool_u"), pool_w, pool_scale)
        oc_conv = conv_branch(partc("conv_x"), partc("conv_b"), partc("conv_c"), conv_w)
        yc = gated_merge([oc_na, oc_pool, oc_conv, oc_ssm],
                         [partc("na_z"), partc("pool_z"), partc("conv_z"), partc("ssm_z")],
                         partc("merge"), b_gate, w_br, w_o)
    return y, yc


def setup_inputs(seed: int = 0) -> dict:
    key = jax.random.key(seed)
    ks = jax.random.split(key, 32)
    f32 = jnp.float32

    def nrm(k, shape, std):
        return jax.random.normal(k, shape, f32) * std

    D = D_MODEL
    n_idx = jnp.arange(SSM_STATE, dtype=f32)
    sp = (DEPTH, 2, SSM_GROUPS, SSM_STATE)
    return {
        "x": nrm(ks[0], (BATCH, SEQ, D), 1.0),
        "c": nrm(ks[1], (BATCH, D), 1.0),
        "ctx": nrm(ks[2], (BATCH, CTX_LEN, D), 1.0),
        "c_ctx": nrm(ks[3], (D,), 1.0),
        "w_mod": nrm(ks[4], (DEPTH, D, 3 * D), 0.5 * D ** -0.5),
        "b_mod": nrm(ks[5], (DEPTH, 3 * D), 0.02),
        "g_pre": 1.0 + nrm(ks[6], (DEPTH, D), 0.02),
        "g_post": 1.0 + nrm(ks[7], (DEPTH, D), 0.02),
        "w_in": nrm(ks[8], (DEPTH, D, IN_TOTAL), D ** -0.5),
        "b_gate": nrm(ks[9], (DEPTH, N_BRANCHES * D), 0.02),
        "na_rpb": nrm(ks[10], (DEPTH, NA_HEADS, 2 * NA_WIN_ROWS - 1, 2 * NA_WIN_COLS - 1), 0.02),
        "pool_w": nrm(ks[11], (DEPTH, POOL_GROUPS, POOL_GROUP_DIM, POOL_GROUP_DIM), POOL_GROUP_DIM ** -0.5),
        "pool_scale": 1.0 + nrm(ks[12], (DEPTH, POOL_WIDTH), 0.02),
        "conv_w": nrm(ks[13], (DEPTH, CONV_K, CONV_WIDTH), CONV_K ** -0.5),
        "ssm_a_re": -0.5 + nrm(ks[14], sp, 0.01),
        "ssm_a_im": math.pi * n_idx + nrm(ks[15], sp, 0.01),
        "ssm_log_dt": jax.random.uniform(ks[16], (DEPTH, 2, SSM_GROUPS), f32,
                                         math.log(SSM_DT_MIN), math.log(SSM_DT_MAX)),
        "ssm_b_re": nrm(ks[17], sp + (SSM_GROUP_DIM,), (2 * SSM_GROUP_DIM) ** -0.5),
        "ssm_b_im": nrm(ks[18], sp + (SSM_GROUP_DIM,), (2 * SSM_GROUP_DIM) ** -0.5),
        "ssm_c_re": nrm(ks[19], (DEPTH, 2, SSM_GROUPS, SSM_GROUP_DIM, SSM_STATE), SSM_STATE ** -0.5),
        "ssm_c_im": nrm(ks[20], (DEPTH, 2, SSM_GROUPS, SSM_GROUP_DIM, SSM_STATE), SSM_STATE ** -0.5),
        "ssm_d": nrm(ks[21], (DEPTH, SSM_WIDTH), 1.0),
        "glu_w": nrm(ks[22], (DEPTH, SSM_WIDTH, 2 * SSM_WIDTH), SSM_WIDTH ** -0.5),
        "w_br": nrm(ks[23], (DEPTH, BRANCH_TOTAL, D), BRANCH_WIDTH ** -0.5),
        "w_o": nrm(ks[24], (DEPTH, D, D), D ** -0.5),
    }


def reference(x, c, ctx, c_ctx, w_mod, b_mod, g_pre, g_post, w_in, b_gate, na_rpb, pool_w,
              pool_scale, conv_w, ssm_a_re, ssm_a_im, ssm_log_dt, ssm_b_re, ssm_b_im,
              ssm_c_re, ssm_c_im, ssm_d, glu_w, w_br, w_o):
    c_act = jax.nn.silu(c)
    cc_act = jax.nn.silu(c_ctx)
    xc = ctx
    for i in range(DEPTH):
        with_ctx_out = i < DEPTH - 1
        shift, scale, gate = jnp.split(c_act @ w_mod[i] + b_mod[i], 3, axis=-1)
        shift_c, scale_c, gate_c = jnp.split(cc_act @ w_mod[i] + b_mod[i], 3, axis=-1)
        h = rms_norm(x, g_pre[i]) * (1.0 + scale[:, None]) + shift[:, None]
        hc = rms_norm(xc, g_pre[i]) * (1.0 + scale_c) + shift_c
        y, yc = hybrid_mixer(h, hc, w_in[i], b_gate[i], na_rpb[i], pool_w[i], pool_scale[i], conv_w[i],
                             ssm_a_re[i], ssm_a_im[i], ssm_log_dt[i], ssm_b_re[i], ssm_b_im[i],
                             ssm_c_re[i], ssm_c_im[i], ssm_d[i], glu_w[i], w_br[i], w_o[i], with_ctx_out)
        x = x + gate[:, None] * rms_norm(y, g_post[i])
        if with_ctx_out:
            xc = xc + gate_c * rms_norm(yc, g_post[i])
    return x
```

```python
import functools
import math

import jax
import jax.numpy as jnp
from jax import lax
from jax.experimental import pallas as pl
from jax.experimental.pallas import tpu as pltpu

F32 = jnp.float32
BF16 = jnp.bfloat16

D_MODEL = 2048
GRID_W = 64
RMS_EPS = 1e-6
NEG_INF = -1e30
BRANCH = 512
N_BRANCHES = 4
NA_HEADS = 8
NA_HEAD_DIM = 64
NA_WIN_ROWS = 8
NA_WIN_COLS = 16
POOL_WINDOWS = (2, 4, 8, 16)
POOL_GROUP_DIM = 128
SSM_GROUPS = 32
SSM_GROUP_DIM = 16
SSM_STATE = 64
IN_TOTAL = 12 * BRANCH + N_BRANCHES * D_MODEL
COL_Q, COL_K, COL_V, COL_NA_Z = 0, 1, 2, 3
COL_POOL_U, COL_POOL_Z = 4, 5
COL_CONV_X, COL_CONV_B, COL_CONV_C, COL_CONV_Z = 6, 7, 8, 9
COL_SSM_U, COL_SSM_Z = 10, 11
COL_MERGE = 12

V7X_LANES = 128
SSM_CHUNK = 8
SSM_LANE_GROUPS = V7X_LANES // SSM_GROUP_DIM
SSM_TILES = BRANCH // V7X_LANES
HALO = 16
VMEM_LIMIT = 56 * 1024 * 1024


def _cparams(sem):
    return pltpu.CompilerParams(dimension_semantics=sem, vmem_limit_bytes=VMEM_LIMIT)


def _const_spec(shape):
    nd = len(shape)
    return pl.BlockSpec(shape, lambda *_: (0,) * nd, pipeline_mode=pl.Buffered(1))


def _mod_kernel(c_ref, w_ref, b_ref, o_ref):
    c = c_ref[...]
    a = (c * jax.nn.sigmoid(c)).astype(BF16)
    o_ref[0] = jnp.dot(a, w_ref[0].astype(BF16), preferred_element_type=F32) + b_ref[0]


def _mod_call(c8, w_mod, b_mod):
    depth, d, n3 = w_mod.shape
    tn = 512
    return pl.pallas_call(
        _mod_kernel,
        grid=(depth, n3 // tn),
        in_specs=[
            pl.BlockSpec((8, d), lambda l, j: (0, 0)),
            pl.BlockSpec((1, d, tn), lambda l, j: (l, 0, j)),
            pl.BlockSpec((1, 1, tn), lambda l, j: (l, 0, j)),
        ],
        out_specs=pl.BlockSpec((1, 8, tn), lambda l, j: (l, 0, j)),
        out_shape=jax.ShapeDtypeStruct((depth, 8, n3), F32),
        compiler_params=_cparams(("arbitrary", "arbitrary")),
        name="mod",
    )(c8, w_mod, b_mod.reshape(depth, 1, n3))


def _inproj_kernel(x_ref, scale_ref, shift_ref, g_ref, w_ref, o_ref, h_ref):
    @pl.when(pl.program_id(1) == 0)
    def _():
        x = x_ref[...]
        ms = jnp.mean(x * x, axis=-1, keepdims=True)
        y = x * lax.rsqrt(ms + RMS_EPS) * g_ref[...]
        h_ref[...] = (y * (1.0 + scale_ref[0]) + shift_ref[0]).astype(BF16)

    o_ref[...] = jnp.dot(h_ref[...], w_ref[...], preferred_element_type=F32).astype(o_ref.dtype)


def _inproj_call(x2, scale, shift, g_pre, w_bf, rows_per_seq, tm, tn):
    r, d = x2.shape
    n = w_bf.shape[1]
    tiles_per_seq = rows_per_seq // tm
    return pl.pallas_call(
        _inproj_kernel,
        grid=(r // tm, n // tn),
        in_specs=[
            pl.BlockSpec((tm, d), lambda i, j: (i, 0)),
            pl.BlockSpec((1, 1, d), lambda i, j: (i // tiles_per_seq, 0, 0)),
            pl.BlockSpec((1, 1, d), lambda i, j: (i // tiles_per_seq, 0, 0)),
            pl.BlockSpec((1, d), lambda i, j: (0, 0)),
            pl.BlockSpec((d, tn), lambda i, j: (0, j)),
        ],
        out_specs=pl.BlockSpec((tm, tn), lambda i, j: (i, j)),
        out_shape=jax.ShapeDtypeStruct((r, n), BF16),
        scratch_shapes=[pltpu.VMEM((tm, d), BF16)],
        compiler_params=_cparams(("arbitrary", "arbitrary")),
        name="inproj",
    )(x2, scale, shift, g_pre, w_bf)


def _head_mask(rows):
    lane = lax.broadcasted_iota(jnp.int32, (rows, V7X_LANES), 1)
    return lane < NA_HEAD_DIM


def _na_kernel(q_ref, k_ref, v_ref, kc_ref, vc_ref, bias_ref, o_ref, *, rq, grid_rows):
    rb = pl.program_id(1)
    first_half = _head_mask(GRID_W)
    band = NA_WIN_ROWS * GRID_W
    scale = NA_HEAD_DIM ** -0.5
    nt = (((1,), (1,)), ((), ()))

    def row_body(i, carry):
        r = rb * rq + i
        start = jnp.clip(r - NA_WIN_ROWS // 2, 0, grid_rows - NA_WIN_ROWS)
        cls = jnp.where(r < NA_WIN_ROWS // 2, r,
                        jnp.where(r > grid_rows - NA_WIN_ROWS // 2,
                                  r - grid_rows + NA_WIN_ROWS, NA_WIN_ROWS // 2))
        k0 = pl.multiple_of(start * GRID_W, GRID_W)
        q0 = pl.multiple_of(i * GRID_W, GRID_W)
        for hp in range(NA_HEADS // 2):
            cs = slice(hp * V7X_LANES, (hp + 1) * V7X_LANES)
            qp = q_ref[pl.ds(q0, GRID_W), cs] * jnp.asarray(scale, BF16)
            kp = k_ref[pl.ds(k0, band), cs]
            vp = v_ref[pl.ds(k0, band), cs]
            kcp = kc_ref[:, cs]
            vcp = vc_ref[:, cs]
            outs = []
            for e in range(2):
                keep = first_half if e == 0 else jnp.logical_not(first_half)
                qm = jnp.where(keep, qp, jnp.zeros_like(qp))
                s_b = lax.dot_general(qm, kp, nt, preferred_element_type=F32) + bias_ref[cls, 2 * hp + e]
                s_c = lax.dot_general(qm, kcp, nt, preferred_element_type=F32)
                m = jnp.maximum(jnp.max(s_b, axis=-1, keepdims=True), jnp.max(s_c, axis=-1, keepdims=True))
                p_b = jnp.exp(s_b - m)
                p_c = jnp.exp(s_c - m)
                denom = jnp.sum(p_b, axis=-1, keepdims=True) + jnp.sum(p_c, axis=-1, keepdims=True)
                o = (jnp.dot(p_b.astype(BF16), vp, preferred_element_type=F32)
                     + jnp.dot(p_c.astype(BF16), vcp, preferred_element_type=F32))
                outs.append(o / denom)
            o_ref[pl.ds(q0, GRID_W), cs] = jnp.where(first_half, outs[0], outs[1]).astype(o_ref.dtype)
        return carry

    lax.fori_loop(0, rq, row_body, 0)


def _na_call(proj, projc, bias_tbl, batch, seq, n_ctx, rq):
    grid_rows = seq // GRID_W
    tq = rq * GRID_W
    qb = seq // tq
    return pl.pallas_call(
        functools.partial(_na_kernel, rq=rq, grid_rows=grid_rows),
        grid=(batch, qb),
        in_specs=[
            pl.BlockSpec((tq, BRANCH), lambda b, r: (b * qb + r, COL_Q)),
            pl.BlockSpec((seq, BRANCH), lambda b, r: (b, COL_K)),
            pl.BlockSpec((seq, BRANCH), lambda b, r: (b, COL_V)),
            pl.BlockSpec((n_ctx, BRANCH), lambda b, r: (b, COL_K)),
            pl.BlockSpec((n_ctx, BRANCH), lambda b, r: (b, COL_V)),
            _const_spec(bias_tbl.shape),
        ],
        out_specs=pl.BlockSpec((tq, BRANCH), lambda b, r: (b * qb + r, 0)),
        out_shape=jax.ShapeDtypeStruct((batch * seq, BRANCH), BF16),
        compiler_params=_cparams(("arbitrary", "arbitrary")),
        name="na_attn",
    )(proj, proj, proj, projc, projc, bias_tbl)


def _ctx_attn_kernel(q_ref, k_ref, v_ref, o_ref):
    n = q_ref.shape[0]
    first_half = _head_mask(n)
    scale = NA_HEAD_DIM ** -0.5
    nt = (((1,), (1,)), ((), ()))
    for hp in range(NA_HEADS // 2):
        cs = slice(hp * V7X_LANES, (hp + 1) * V7X_LANES)
        qp = q_ref[:, cs] * jnp.asarray(scale, BF16)
        kp = k_ref[:, cs]
        vp = v_ref[:, cs]
        outs = []
        for e in range(2):
            keep = first_half if e == 0 else jnp.logical_not(first_half)
            qm = jnp.where(keep, qp, jnp.zeros_like(qp))
            s = lax.dot_general(qm, kp, nt, preferred_element_type=F32)
            m = jnp.max(s, axis=-1, keepdims=True)
            p = jnp.exp(s - m)
            denom = jnp.sum(p, axis=-1, keepdims=True)
            outs.append(jnp.dot(p.astype(BF16), vp, preferred_element_type=F32) / denom)
        o_ref[:, cs] = jnp.where(first_half, outs[0], outs[1]).astype(o_ref.dtype)


def _ctx_attn_call(projc, batch, n_ctx):
    return pl.pallas_call(
        _ctx_attn_kernel,
        grid=(batch,),
        in_specs=[
            pl.BlockSpec((n_ctx, BRANCH), lambda b: (b, COL_Q)),
            pl.BlockSpec((n_ctx, BRANCH), lambda b: (b, COL_K)),
            pl.BlockSpec((n_ctx, BRANCH), lambda b: (b, COL_V)),
        ],
        out_specs=pl.BlockSpec((n_ctx, BRANCH), lambda b: (b, 0)),
        out_shape=jax.ShapeDtypeStruct((batch * n_ctx, BRANCH), BF16),
        compiler_params=_cparams(("arbitrary",)),
        name="ctx_attn",
    )(projc, projc, projc)


def _na_bias_table(rpb):
    cls = jnp.arange(NA_WIN_ROWS)
    kr = jnp.arange(NA_WIN_ROWS)
    drow = kr[None, :] - cls[:, None] + (NA_WIN_ROWS - 1)
    col = jnp.arange(GRID_W)
    col_start = jnp.clip(col - NA_WIN_COLS // 2, 0, GRID_W - NA_WIN_COLS)
    in_win = (col[None, :] >= col_start[:, None]) & (col[None, :] < col_start[:, None] + NA_WIN_COLS)
    dcol = jnp.clip(col[None, :] - col[:, None] + (NA_WIN_COLS - 1), 0, 2 * NA_WIN_COLS - 2)
    b = rpb[:, drow[:, None, :, None], dcol[None, :, None, :]].astype(F32)
    b = jnp.where(in_win[None, None, :, None, :], b, NEG_INF)
    return b.transpose(1, 0, 2, 3, 4).reshape(NA_WIN_ROWS, NA_HEADS, GRID_W, NA_WIN_ROWS * GRID_W)


def _cmul(ar, ai, br, bi):
    return ar * br - ai * bi, ar * bi + ai * br


def _ssm_weights(a_re, a_im, log_dt, b_re, b_im, c_re, c_im):
    t_len = SSM_CHUNK
    g, p, hg = SSM_GROUPS, SSM_STATE, SSM_GROUP_DIM
    lg = SSM_LANE_GROUPS
    kf, ef, mf, dec = [], [], [], []
    for d in range(2):
        ar, ai = a_re[d].astype(F32), a_im[d].astype(F32)
        dt = jnp.exp(log_dt[d].astype(F32))[:, None]
        mag = jnp.exp(ar * dt)
        abr, abi = mag * jnp.cos(ai * dt), mag * jnp.sin(ai * dt)
        den = ar * ar + ai * ai
        nr, ni = abr - 1.0, abi
        fr = (nr * ar + ni * ai) / den
        fi = (ni * ar - nr * ai) / den
        bbr, bbi = _cmul(fr[..., None], fi[..., None], b_re[d].astype(F32), b_im[d].astype(F32))
        pws = [(jnp.ones_like(abr), jnp.zeros_like(abr))]
        for _ in range(t_len):
            pws.append(_cmul(pws[-1][0], pws[-1][1], abr, abi))
        pr = jnp.stack([x[0] for x in pws])
        pi = jnp.stack([x[1] for x in pws])
        cr, ci = c_re[d].astype(F32), c_im[d].astype(F32)
        mr, mi = _cmul(cr[None], ci[None], pr[:t_len, :, None, :], pi[:t_len, :, None, :])
        kf.append(jnp.einsum('tghp,gpk->tghk', mr, bbr) - jnp.einsum('tghp,gpk->tghk', mi, bbi))
        tt = jnp.arange(t_len)
        pw_idx = (t_len - 1 - tt) if d == 0 else tt
        er, ei = _cmul(pr[pw_idx][..., None], pi[pw_idx][..., None], bbr[None], bbi[None])
        ef.append((er, ei))
        out_idx = (tt + 1) if d == 0 else (t_len - tt)
        m2r, m2i = _cmul(cr[None], ci[None], pr[out_idx][:, :, None, :], pi[out_idx][:, :, None, :])
        mf.append((m2r, m2i))
        dec.append((pr[t_len], pi[t_len]))
    eye = jnp.eye(lg, dtype=F32)
    lag = jnp.arange(t_len)[None, :] - jnp.arange(t_len)[:, None]
    kfw = jnp.where((lag >= 0)[:, :, None, None, None], kf[0][jnp.clip(lag, 0, t_len - 1)], 0.0)
    kbw = jnp.where((lag <= 0)[:, :, None, None, None], kf[1][jnp.clip(-lag, 0, t_len - 1)], 0.0)
    kfull = (kfw + kbw).reshape(t_len, t_len, SSM_TILES, lg, hg, hg)
    w_intra = jnp.einsum('absghi,gf->safibgh', kfull, eye)
    w_intra = w_intra.reshape(SSM_TILES, t_len * V7X_LANES, t_len * V7X_LANES)
    parts = []
    for d in range(2):
        for comp in ef[d]:
            c5 = comp.reshape(t_len, SSM_TILES, lg, p, hg)
            w = jnp.einsum('asgpi,gf->safigp', c5, eye)
            parts.append(w.reshape(SSM_TILES, t_len * V7X_LANES, lg * p))
    w1 = jnp.concatenate([w_intra] + parts, axis=-1)
    rows = []
    for d in range(2):
        m2r, m2i = mf[d]
        for comp in (m2r, -m2i):
            c5 = comp.reshape(t_len, SSM_TILES, lg, hg, p)
            w = jnp.einsum('bsghp,gf->sfpbgh', c5, eye)
            rows.append(w.reshape(SSM_TILES, lg * p, t_len * V7X_LANES))
    wc = jnp.concatenate(rows, axis=1)
    decay = jnp.stack([x.reshape(SSM_TILES, lg * p) for d in range(2) for x in dec[d]], axis=1)
    return w1.astype(BF16), wc.astype(BF16), decay


def _ssm_kernel(uc_ref, u_ref, w1_ref, wc_ref, dec_ref, dsk_ref, yc_ref, y_ref,
                uf_ref, lhs_ref, yi_ref, es_ref, *, n_ctx, seq):
    t_len = SSM_CHUNK
    nc, lc = n_ctx // t_len, seq // t_len
    nj = nc + lc
    kdim = t_len * V7X_LANES
    sw = SSM_LANE_GROUPS * SSM_STATE
    uf_ref[0:n_ctx, :] = uc_ref[...].astype(F32)
    uf_ref[n_ctx:n_ctx + seq, :] = u_ref[...].astype(F32)
    for t in range(t_len):
        lhs_ref[:, t * V7X_LANES:(t + 1) * V7X_LANES] = uf_ref[pl.ds(t, nj, stride=t_len), :]
    lhs = lhs_ref[...].astype(BF16)
    yi_ref[...] = jnp.dot(lhs, w1_ref[0, :, 0:kdim], preferred_element_type=F32)
    es_ref[:, 0:2 * sw] = jnp.dot(lhs, w1_ref[0, :, kdim:kdim + 2 * sw], preferred_element_type=F32)
    es_ref[:, 2 * sw:4 * sw] = jnp.dot(lhs, w1_ref[0, :, kdim + 2 * sw:kdim + 4 * sw], preferred_element_type=F32)

    dec = dec_ref[0]
    afr, afi, abr, abi = dec[0:1], dec[1:2], dec[2:3], dec[3:4]

    def step(k, st):
        sfr, sfi, sbr, sbi = st
        jb = jnp.where(k < nc, nc - 1 - k, nj - 1 - (k - nc))
        efr = es_ref[pl.ds(k, 1), 0:sw]
        efi = es_ref[pl.ds(k, 1), sw:2 * sw]
        ebr = es_ref[pl.ds(jb, 1), 2 * sw:3 * sw]
        ebi = es_ref[pl.ds(jb, 1), 3 * sw:4 * sw]
        es_ref[pl.ds(k, 1), 0:sw] = sfr
        es_ref[pl.ds(k, 1), sw:2 * sw] = sfi
        es_ref[pl.ds(jb, 1), 2 * sw:3 * sw] = sbr
        es_ref[pl.ds(jb, 1), 3 * sw:4 * sw] = sbi
        nfr, nfi = _cmul(afr, afi, sfr, sfi)
        nbr, nbi = _cmul(abr, abi, sbr, sbi)
        return nfr + efr, nfi + efi, nbr + ebr, nbi + ebi

    z = jnp.zeros((1, sw), F32)
    lax.fori_loop(0, nj, step, (z, z, z, z))

    y = (yi_ref[...] + jnp.dot(es_ref[...].astype(BF16), wc_ref[0], preferred_element_type=F32)
         + lhs_ref[...] * dsk_ref[0])
    yi_ref[...] = y
    for t in range(t_len):
        uf_ref[pl.ds(t, nj, stride=t_len), :] = yi_ref[:, t * V7X_LANES:(t + 1) * V7X_LANES]
    yc_ref[...] = uf_ref[0:n_ctx, :]
    y_ref[...] = uf_ref[n_ctx:n_ctx + seq, :]


def _ssm_call(proj, projc, w1, wc, decay, dsk_tiled, batch, seq, n_ctx):
    t_len = SSM_CHUNK
    nj = (n_ctx + seq) // t_len
    kdim = t_len * V7X_LANES
    sw4 = 4 * SSM_LANE_GROUPS * SSM_STATE
    col0 = COL_SSM_U * BRANCH // V7X_LANES
    return pl.pallas_call(
        functools.partial(_ssm_kernel, n_ctx=n_ctx, seq=seq),
        grid=(SSM_TILES, batch),
        in_specs=[
            pl.BlockSpec((n_ctx, V7X_LANES), lambda s, b: (b, col0 + s)),
            pl.BlockSpec((seq, V7X_LANES), lambda s, b: (b, col0 + s)),
            pl.BlockSpec((1, kdim, kdim + sw4), lambda s, b: (s, 0, 0)),
            pl.BlockSpec((1, sw4, kdim), lambda s, b: (s, 0, 0)),
            pl.BlockSpec((1, 4, sw4 // 4), lambda s, b: (s, 0, 0)),
            pl.BlockSpec((1, 1, kdim), lambda s, b: (s, 0, 0)),
        ],
        out_specs=[
            pl.BlockSpec((n_ctx, V7X_LANES), lambda s, b: (b, s)),
            pl.BlockSpec((seq, V7X_LANES), lambda s, b: (b, s)),
        ],
        out_shape=[
            jax.ShapeDtypeStruct((batch * n_ctx, BRANCH), F32),
            jax.ShapeDtypeStruct((batch * seq, BRANCH), F32),
        ],
        scratch_shapes=[
            pltpu.VMEM((n_ctx + seq, V7X_LANES), F32),
            pltpu.VMEM((nj, kdim), F32),
            pltpu.VMEM((nj, kdim), F32),
            pltpu.VMEM((nj, sw4), F32),
        ],
        compiler_params=_cparams(("arbitrary", "arbitrary")),
        name="ssm",
    )(projc, proj, w1, wc, decay, dsk_tiled)


def _merge_kernel(x_ref, gate_ref, gpost_ref,
                  naz_ref, pu_ref, pz_ref, cx_ref, cb_ref, cc_ref, cz_ref, sz_ref,
                  lg0_ref, lg1_ref, lg2_ref, lg3_ref,
                  pu_prev_ref, pu_next_ref, cx_prev_ref, cx_next_ref, cc_prev_ref, cc_next_ref,
                  ona_ref, yssm_ref,
                  poolw_ref, pscale_ref, convw_ref, gluw_ref, bgate_ref, wbr_ref, wo_ref,
                  o_ref, pad_ref, *, tm, rows_per_seq):
    i = pl.program_id(0)
    tiles_per_seq = rows_per_seq // tm
    ti = i % tiles_per_seq
    has_prev = ti > 0
    has_next = ti < tiles_per_seq - 1
    t0 = ti * tm
    pos = t0 + lax.broadcasted_iota(jnp.int32, (tm, 1), 0)

    pad_ref[0:HALO, :] = jnp.where(has_prev, pu_prev_ref[...].astype(F32), 0.0)
    pad_ref[HALO:HALO + tm, :] = pu_ref[...].astype(F32)
    pad_ref[HALO + tm:2 * HALO + tm, :] = jnp.where(has_next, pu_next_ref[...].astype(F32), 0.0)
    mixed = []
    for gi, w in enumerate(POOL_WINDOWS):
        cs = slice(gi * POOL_GROUP_DIM, (gi + 1) * POOL_GROUP_DIM)
        acc = None
        for dlt in range(-(w // 2), w - w // 2):
            v = pad_ref[HALO + dlt:HALO + dlt + tm, cs]
            acc = v if acc is None else acc + v
        lo = jnp.maximum(pos - w // 2, 0)
        hi = jnp.minimum(pos + w - w // 2, rows_per_seq)
        cnt = (hi - lo).astype(F32)
        pooled = acc / cnt - pad_ref[HALO:HALO + tm, cs]
        mixed.append(jnp.dot(pooled.astype(BF16), poolw_ref[gi], preferred_element_type=F32))
    o_pool = jnp.concatenate(mixed, axis=-1) * pscale_ref[...]

    pad_ref[0:HALO, :] = jnp.where(
        has_prev, cc_prev_ref[...].astype(F32) * cx_prev_ref[...].astype(F32), 0.0)
    pad_ref[HALO:HALO + tm, :] = cc_ref[...].astype(F32) * cx_ref[...].astype(F32)
    pad_ref[HALO + tm:2 * HALO + tm, :] = jnp.where(
        has_next, cc_next_ref[...].astype(F32) * cx_next_ref[...].astype(F32), 0.0)
    cw = convw_ref[...]
    conv = (pad_ref[HALO - 1:HALO - 1 + tm, :] * cw[0:1] + pad_ref[HALO:HALO + tm, :] * cw[1:2]
            + pad_ref[HALO + 1:HALO + 1 + tm, :] * cw[2:3])
    o_conv = cb_ref[...].astype(F32) * conv

    g = jax.nn.gelu(yssm_ref[...]).astype(BF16)
    gg = jnp.dot(g, gluw_ref[...], preferred_element_type=F32)
    o_ssm = gg[:, 0:BRANCH] * jax.nn.sigmoid(gg[:, BRANCH:2 * BRANCH])

    outs = (ona_ref[...].astype(F32), o_pool, o_conv, o_ssm)
    zs = (naz_ref, pz_ref, cz_ref, sz_ref)
    lgs = (lg0_ref, lg1_ref, lg2_ref, lg3_ref)
    merged = None
    for bi in range(N_BRANCHES):
        z = zs[bi][...].astype(F32)
        a = (outs[bi] * (z * jax.nn.sigmoid(z))).astype(BF16)
        br = jnp.dot(a, wbr_ref[bi * BRANCH:(bi + 1) * BRANCH, :], preferred_element_type=F32)
        gt = jax.nn.sigmoid(lgs[bi][...].astype(F32) + bgate_ref[:, bi * D_MODEL:(bi + 1) * D_MODEL])
        term = gt * br
        merged = term if merged is None else merged + term
    y = jnp.dot(merged.astype(BF16), wo_ref[...], preferred_element_type=F32)
    ms = jnp.mean(y * y, axis=-1, keepdims=True)
    yn = y * lax.rsqrt(ms + RMS_EPS) * gpost_ref[...]
    o_ref[...] = x_ref[...] + gate_ref[0] * yn


def _merge_call(x2, gate, g_post, proj, o_na, y_ssm, pool_w, pool_scale, conv_w, glu_w, b_gate,
                w_br, w_o, rows_per_seq, tm):
    r, d = x2.shape
    tiles_per_seq = rows_per_seq // tm
    hb = tm // HALO
    n_halo_blocks = r // HALO

    def col(c):
        return pl.BlockSpec((tm, BRANCH), lambda i: (i, c))

    def lg(c):
        return pl.BlockSpec((tm, D_MODEL), lambda i: (i, COL_MERGE * BRANCH // D_MODEL + c))

    def prev(c):
        return pl.BlockSpec((HALO, BRANCH), lambda i: (jnp.maximum(i * hb - 1, 0), c))

    def nxt(c):
        return pl.BlockSpec((HALO, BRANCH), lambda i: (jnp.minimum((i + 1) * hb, n_halo_blocks - 1), c))

    in_specs = [
        pl.BlockSpec((tm, d), lambda i: (i, 0)),
        pl.BlockSpec((1, 1, d), lambda i: (i // tiles_per_seq, 0, 0)),
        pl.BlockSpec((1, d), lambda i: (0, 0)),
        col(COL_NA_Z), col(COL_POOL_U), col(COL_POOL_Z), col(COL_CONV_X), col(COL_CONV_B),
        col(COL_CONV_C), col(COL_CONV_Z), col(COL_SSM_Z),
        lg(0), lg(1), lg(2), lg(3),
        prev(COL_POOL_U), nxt(COL_POOL_U), prev(COL_CONV_X), nxt(COL_CONV_X),
        prev(COL_CONV_C), nxt(COL_CONV_C),
        pl.BlockSpec((tm, BRANCH), lambda i: (i, 0)),
        pl.BlockSpec((tm, BRANCH), lambda i: (i, 0)),
        _const_spec(pool_w.shape), _const_spec(pool_scale.shape), _const_spec(conv_w.shape),
        _const_spec(glu_w.shape), _const_spec(b_gate.shape), _const_spec(w_br.shape),
        _const_spec(w_o.shape),
    ]
    args = [x2, gate, g_post] + [proj] * 8 + [proj] * 4 + [proj] * 6 + [
        o_na, y_ssm, pool_w, pool_scale, conv_w, glu_w, b_gate, w_br, w_o]
    return pl.pallas_call(
        functools.partial(_merge_kernel, tm=tm, rows_per_seq=rows_per_seq),
        grid=(r // tm,),
        in_specs=in_specs,
        out_specs=pl.BlockSpec((tm, d), lambda i: (i, 0)),
        out_shape=jax.ShapeDtypeStruct((r, d), F32),
        scratch_shapes=[pltpu.VMEM((tm + 2 * HALO, BRANCH), F32)],
        compiler_params=_cparams(("arbitrary",)),
        name="merge",
    )(*args)


def _tile_rows(rows, pref):
    t = min(rows, pref)
    assert rows % t == 0
    return t


def kernel(x, c, ctx, c_ctx, w_mod, b_mod, g_pre, g_post, w_in, b_gate, na_rpb, pool_w, pool_scale,
           conv_w, ssm_a_re, ssm_a_im, ssm_log_dt, ssm_b_re, ssm_b_im, ssm_c_re, ssm_c_im, ssm_d,
           glu_w, w_br, w_o):
    batch, seq, d = x.shape
    n_ctx = ctx.shape[1]
    depth = w_mod.shape[0]
    assert d == D_MODEL and seq % (GRID_W * NA_WIN_ROWS) == 0 and batch + 1 <= 8
    assert n_ctx % HALO == 0 and w_in.shape[-1] == IN_TOTAL

    c8 = jnp.concatenate([c, c_ctx[None], jnp.zeros((7 - batch, d), F32)], axis=0)
    mod = _mod_call(c8, w_mod, b_mod)

    x2 = x.reshape(batch * seq, d)
    xc2 = ctx.reshape(batch * n_ctx, d)
    tm_in = _tile_rows(seq, 1024)
    tm_in_c = _tile_rows(batch * n_ctx, 1024)
    tm_mg = _tile_rows(seq, 256)
    tm_mg_c = _tile_rows(n_ctx, 256)
    rq = 8

    for i in range(depth):
        with_ctx_out = i < depth - 1
        shift, scale, gate = (mod[i, :, k * d:(k + 1) * d] for k in range(3))
        lat = lambda v: v[:batch].reshape(batch, 1, d)
        cx = lambda v: v[batch:batch + 1].reshape(1, 1, d)
        w_bf = w_in[i].astype(BF16)
        gp = g_pre[i].reshape(1, d)
        gq = g_post[i].reshape(1, d)

        proj = _inproj_call(x2, lat(scale), lat(shift), gp, w_bf, seq, tm_in, 1024)
        projc = _inproj_call(xc2, cx(scale), cx(shift), gp, w_bf, batch * n_ctx, tm_in_c, 1024)

        bias_tbl = _na_bias_table(na_rpb[i])
        o_na = _na_call(proj, projc, bias_tbl, batch, seq, n_ctx, rq)

        w1, wc, decay = _ssm_weights(ssm_a_re[i], ssm_a_im[i], ssm_log_dt[i], ssm_b_re[i], ssm_b_im[i],
                                     ssm_c_re[i], ssm_c_im[i])
        dsk = jnp.tile(ssm_d[i].astype(F32).reshape(SSM_TILES, 1, V7X_LANES), (1, 1, SSM_CHUNK))
        yc_ssm, y_ssm = _ssm_call(proj, projc, w1, wc, decay, dsk, batch, seq, n_ctx)

        mw = (pool_w[i].astype(BF16), pool_scale[i].reshape(1, BRANCH), conv_w[i],
              glu_w[i].astype(BF16), b_gate[i].reshape(1, N_BRANCHES * d),
              w_br[i].astype(BF16), w_o[i].astype(BF16))
        x2 = _merge_call(x2, lat(gate), gq, proj, o_na, y_ssm, *mw, seq, tm_mg)
        if with_ctx_out:
            oc_na = _ctx_attn_call(projc, batch, n_ctx)
            gate_c = jnp.broadcast_to(cx(gate), (batch, 1, d))
            xc2 = _merge_call(xc2, gate_c, gq, projc, oc_na, yc_ssm, *mw, n_ctx, tm_mg_c)
    return x2.reshape(batch, seq, d)
```

```python
import functools

import jax
import jax.numpy as jnp
import numpy as np
from jax import lax
from jax.experimental import pallas as pl
from jax.experimental.pallas import tpu as pltpu

F32 = jnp.float32
BF16 = jnp.bfloat16

D_MODEL = 2048
GRID_W = 64
RMS_EPS = 1e-6
NEG_INF = -1e30
BRANCH = 512
N_BRANCHES = 4
NA_HEADS = 8
NA_HEAD_DIM = 64
NA_WIN_ROWS = 8
NA_WIN_COLS = 16
NA_GROUP_ROWS = 4
NA_KEY_ROWS = 12
POOL_WINDOWS = (2, 4, 8, 16)
POOL_GROUP_DIM = 128
SSM_GROUPS = 32
SSM_GROUP_DIM = 16
SSM_STATE = 64
IN_TOTAL = 12 * BRANCH + N_BRANCHES * D_MODEL
COL_Q, COL_K, COL_V, COL_NA_Z = 0, 1, 2, 3
COL_POOL_U, COL_POOL_Z = 4, 5
COL_CONV_X, COL_CONV_B, COL_CONV_C, COL_CONV_Z = 6, 7, 8, 9
COL_SSM_U, COL_SSM_Z = 10, 11
COL_MERGE = 12

V7X_LANES = 128
SSM_CHUNK = 8
SSM_LANE_GROUPS = V7X_LANES // SSM_GROUP_DIM
SSM_TILES = BRANCH // V7X_LANES
HALO = 16
VMEM_LIMIT = 56 * 1024 * 1024


def _cparams(sem):
    return pltpu.CompilerParams(dimension_semantics=sem, vmem_limit_bytes=VMEM_LIMIT)


def _const_spec(shape):
    nd = len(shape)
    return pl.BlockSpec(shape, lambda *_: (0,) * nd, pipeline_mode=pl.Buffered(1))


def _mod_kernel(c_ref, w_ref, b_ref, o_ref):
    c = c_ref[...]
    a = (c * jax.nn.sigmoid(c)).astype(BF16)
    o_ref[0] = jnp.dot(a, w_ref[0].astype(BF16), preferred_element_type=F32) + b_ref[0]


def _mod_call(c8, w_mod, b_mod):
    depth, d, n3 = w_mod.shape
    tn = 512
    return pl.pallas_call(
        _mod_kernel,
        grid=(depth, n3 // tn),
        in_specs=[
            pl.BlockSpec((8, d), lambda l, j: (0, 0)),
            pl.BlockSpec((1, d, tn), lambda l, j: (l, 0, j)),
            pl.BlockSpec((1, 1, tn), lambda l, j: (l, 0, j)),
        ],
        out_specs=pl.BlockSpec((1, 8, tn), lambda l, j: (l, 0, j)),
        out_shape=jax.ShapeDtypeStruct((depth, 8, n3), F32),
        compiler_params=_cparams(("arbitrary", "arbitrary")),
        name="mod",
    )(c8, w_mod, b_mod.reshape(depth, 1, n3))


def _inproj_kernel(x_ref, scale_ref, shift_ref, g_ref, w_ref, o_ref, h_ref):
    @pl.when(pl.program_id(1) == 0)
    def _():
        x = x_ref[...]
        ms = jnp.mean(x * x, axis=-1, keepdims=True)
        y = x * lax.rsqrt(ms + RMS_EPS) * g_ref[...]
        h_ref[...] = (y * (1.0 + scale_ref[0]) + shift_ref[0]).astype(BF16)

    o_ref[...] = jnp.dot(h_ref[...], w_ref[...], preferred_element_type=F32).astype(o_ref.dtype)


def _inproj_call(x2, scale, shift, g_pre, w_bf, rows_per_seq, tm, tn):
    r, d = x2.shape
    n = w_bf.shape[1]
    tiles_per_seq = rows_per_seq // tm
    return pl.pallas_call(
        _inproj_kernel,
        grid=(r // tm, n // tn),
        in_specs=[
            pl.BlockSpec((tm, d), lambda i, j: (i, 0)),
            pl.BlockSpec((1, 1, d), lambda i, j: (i // tiles_per_seq, 0, 0)),
            pl.BlockSpec((1, 1, d), lambda i, j: (i // tiles_per_seq, 0, 0)),
            pl.BlockSpec((1, d), lambda i, j: (0, 0)),
            pl.BlockSpec((d, tn), lambda i, j: (0, j)),
        ],
        out_specs=pl.BlockSpec((tm, tn), lambda i, j: (i, j)),
        out_shape=jax.ShapeDtypeStruct((r, n), BF16),
        scratch_shapes=[pltpu.VMEM((tm, d), BF16)],
        compiler_params=_cparams(("arbitrary", "arbitrary")),
        name="inproj",
    )(x2, scale, shift, g_pre, w_bf)


def _head_mask(rows):
    lane = lax.broadcasted_iota(jnp.int32, (rows, V7X_LANES), 1)
    return lane < NA_HEAD_DIM


def _na_kernel(q_ref, k_ref, v_ref, kc_ref, vc_ref, bias_ref, o_ref, *, grid_rows):
    r0 = pl.program_id(1) * NA_GROUP_ROWS
    nq = NA_GROUP_ROWS * GRID_W
    first_half = _head_mask(nq)
    scale = NA_HEAD_DIM ** -0.5
    nt = (((1,), (1,)), ((), ()))
    gs = jnp.clip(r0 - NA_WIN_ROWS // 2, 0, grid_rows - NA_KEY_ROWS)
    cls = jnp.where(r0 == 0, 0, jnp.where(r0 == grid_rows - NA_GROUP_ROWS, 2, 1))
    k0 = pl.multiple_of(gs * GRID_W, GRID_W)
    for hp in range(NA_HEADS // 2):
        cs = slice(hp * V7X_LANES, (hp + 1) * V7X_LANES)
        qp = q_ref[:, cs] * jnp.asarray(scale, BF16)
        kp = k_ref[pl.ds(k0, NA_KEY_ROWS * GRID_W), cs]
        vp = v_ref[pl.ds(k0, NA_KEY_ROWS * GRID_W), cs]
        kcp = kc_ref[:, cs]
        vcp = vc_ref[:, cs]
        zero = jnp.zeros_like(qp)
        q2 = jnp.concatenate([jnp.where(first_half, qp, zero), jnp.where(first_half, zero, qp)], axis=0)
        s_b = lax.dot_general(q2, kp, nt, preferred_element_type=F32) + bias_ref[cls, hp]
        s_c = lax.dot_general(q2, kcp, nt, preferred_element_type=F32)
        m = jnp.maximum(jnp.max(s_b, axis=-1, keepdims=True), jnp.max(s_c, axis=-1, keepdims=True))
        p_b = jnp.exp(s_b - m)
        p_c = jnp.exp(s_c - m)
        denom = jnp.sum(p_b, axis=-1, keepdims=True) + jnp.sum(p_c, axis=-1, keepdims=True)
        o = (jnp.dot(p_b.astype(BF16), vp, preferred_element_type=F32)
             + jnp.dot(p_c.astype(BF16), vcp, preferred_element_type=F32)) / denom
        o_ref[:, cs] = jnp.where(first_half, o[0:nq], o[nq:]).astype(o_ref.dtype)


def _na_call(proj, projc, bias_tbl, batch, seq, n_ctx):
    grid_rows = seq // GRID_W
    assert grid_rows % NA_GROUP_ROWS == 0 and grid_rows >= NA_KEY_ROWS
    tq = NA_GROUP_ROWS * GRID_W
    qb = seq // tq
    return pl.pallas_call(
        functools.partial(_na_kernel, grid_rows=grid_rows),
        grid=(batch, qb),
        in_specs=[
            pl.BlockSpec((tq, BRANCH), lambda b, r: (b * qb + r, COL_Q)),
            pl.BlockSpec((seq, BRANCH), lambda b, r: (b, COL_K)),
            pl.BlockSpec((seq, BRANCH), lambda b, r: (b, COL_V)),
            pl.BlockSpec((n_ctx, BRANCH), lambda b, r: (b, COL_K)),
            pl.BlockSpec((n_ctx, BRANCH), lambda b, r: (b, COL_V)),
            _const_spec(bias_tbl.shape),
        ],
        out_specs=pl.BlockSpec((tq, BRANCH), lambda b, r: (b * qb + r, 0)),
        out_shape=jax.ShapeDtypeStruct((batch * seq, BRANCH), BF16),
        compiler_params=_cparams(("arbitrary", "arbitrary")),
        name="na_attn",
    )(proj, proj, proj, projc, projc, bias_tbl)


def _ctx_attn_kernel(q_ref, k_ref, v_ref, o_ref):
    n = q_ref.shape[0]
    first_half = _head_mask(n)
    scale = NA_HEAD_DIM ** -0.5
    nt = (((1,), (1,)), ((), ()))
    for hp in range(NA_HEADS // 2):
        cs = slice(hp * V7X_LANES, (hp + 1) * V7X_LANES)
        qp = q_ref[:, cs] * jnp.asarray(scale, BF16)
        kp = k_ref[:, cs]
        vp = v_ref[:, cs]
        zero = jnp.zeros_like(qp)
        q2 = jnp.concatenate([jnp.where(first_half, qp, zero), jnp.where(first_half, zero, qp)], axis=0)
        s = lax.dot_general(q2, kp, nt, preferred_element_type=F32)
        m = jnp.max(s, axis=-1, keepdims=True)
        p = jnp.exp(s - m)
        denom = jnp.sum(p, axis=-1, keepdims=True)
        o = jnp.dot(p.astype(BF16), vp, preferred_element_type=F32) / denom
        o_ref[:, cs] = jnp.where(first_half, o[0:n], o[n:]).astype(o_ref.dtype)


def _ctx_attn_call(projc, batch, n_ctx):
    return pl.pallas_call(
        _ctx_attn_kernel,
        grid=(batch,),
        in_specs=[
            pl.BlockSpec((n_ctx, BRANCH), lambda b: (b, COL_Q)),
            pl.BlockSpec((n_ctx, BRANCH), lambda b: (b, COL_K)),
            pl.BlockSpec((n_ctx, BRANCH), lambda b: (b, COL_V)),
        ],
        out_specs=pl.BlockSpec((n_ctx, BRANCH), lambda b: (b, 0)),
        out_shape=jax.ShapeDtypeStruct((batch * n_ctx, BRANCH), BF16),
        compiler_params=_cparams(("arbitrary",)),
        name="ctx_attn",
    )(projc, projc, projc)


def _na_bias_table(rpb):
    col = np.arange(GRID_W)
    col_start = np.clip(col - NA_WIN_COLS // 2, 0, GRID_W - NA_WIN_COLS)
    in_win = (col[None, :] >= col_start[:, None]) & (col[None, :] < col_start[:, None] + NA_WIN_COLS)
    dcol = np.clip(col[None, :] - col[:, None] + (NA_WIN_COLS - 1), 0, 2 * NA_WIN_COLS - 2)
    onehot = (dcol[..., None] == np.arange(2 * NA_WIN_COLS - 1)).astype(np.float32)
    t = jnp.einsum('hdc,qwc->hqdw', rpb.astype(F32), onehot, precision=lax.Precision.HIGHEST)
    t = jnp.where(in_win[None, :, None, :], t, NEG_INF)
    neg = jnp.full((NA_HEADS, GRID_W, 1, GRID_W), NEG_INF, F32)
    slabs = []
    for c in range(3):
        for i in range(NA_GROUP_ROWS):
            lo = (0, i, NA_KEY_ROWS - NA_WIN_ROWS)[c]
            off = (NA_WIN_ROWS - 1 - i, NA_WIN_ROWS // 2 - 1 - i, NA_WIN_ROWS // 2 - 1 - lo - i)[c]
            hi = lo + NA_WIN_ROWS
            slabs.append(jnp.concatenate(
                [neg] * lo + [t[:, :, lo + off:hi + off]] + [neg] * (NA_KEY_ROWS - hi), axis=2))
    b = jnp.stack(slabs).reshape(3, NA_GROUP_ROWS, NA_HEADS // 2, 2, GRID_W, NA_KEY_ROWS * GRID_W)
    b = b.transpose(0, 2, 3, 1, 4, 5)
    return b.reshape(3, NA_HEADS // 2, 2 * NA_GROUP_ROWS * GRID_W, NA_KEY_ROWS * GRID_W)


def _cmul(ar, ai, br, bi):
    return ar * br - ai * bi, ar * bi + ai * br


def _ssm_weights(a_re, a_im, log_dt, b_re, b_im, c_re, c_im):
    t_len = SSM_CHUNK
    p, hg = SSM_STATE, SSM_GROUP_DIM
    lg = SSM_LANE_GROUPS
    kf, ef, mf, dec = [], [], [], []
    for d in range(2):
        ar, ai = a_re[d].astype(F32), a_im[d].astype(F32)
        dt = jnp.exp(log_dt[d].astype(F32))[:, None]
        mag = jnp.exp(ar * dt)
        abr, abi = mag * jnp.cos(ai * dt), mag * jnp.sin(ai * dt)
        den = ar * ar + ai * ai
        nr, ni = abr - 1.0, abi
        fr = (nr * ar + ni * ai) / den
        fi = (ni * ar - nr * ai) / den
        bbr, bbi = _cmul(fr[..., None], fi[..., None], b_re[d].astype(F32), b_im[d].astype(F32))
        pws = [(jnp.ones_like(abr), jnp.zeros_like(abr))]
        for _ in range(t_len):
            pws.append(_cmul(pws[-1][0], pws[-1][1], abr, abi))
        pr = jnp.stack([x[0] for x in pws])
        pi = jnp.stack([x[1] for x in pws])
        cr, ci = c_re[d].astype(F32), c_im[d].astype(F32)
        mr, mi = _cmul(cr[None], ci[None], pr[:t_len, :, None, :], pi[:t_len, :, None, :])
        kf.append(jnp.einsum('tghp,gpk->tghk', mr, bbr) - jnp.einsum('tghp,gpk->tghk', mi, bbi))
        per, pei = (pr[:t_len][::-1], pi[:t_len][::-1]) if d == 0 else (pr[:t_len], pi[:t_len])
        er, ei = _cmul(per[..., None], pei[..., None], bbr[None], bbi[None])
        ef.append((er, ei))
        pcr, pci = (pr[1:], pi[1:]) if d == 0 else (pr[1:][::-1], pi[1:][::-1])
        m2r, m2i = _cmul(cr[None], ci[None], pcr[:, :, None, :], pci[:, :, None, :])
        mf.append((m2r, m2i))
        dec.append((pr[t_len], pi[t_len]))
    lane_g = np.arange(V7X_LANES) // hg
    mask_gh = (lane_g[None, :] == np.arange(lg)[:, None]).astype(np.float32)
    mask_gp = (np.arange(lg * p)[None, :] // p == np.arange(lg)[:, None]).astype(np.float32)
    kc = [k.reshape(t_len, SSM_TILES, lg, hg, hg).transpose(0, 1, 4, 2, 3).reshape(
        t_len, SSM_TILES, hg, V7X_LANES) for k in kf]
    z = jnp.concatenate([kc[1][1:][::-1], kc[0][0:1] + kc[1][0:1], kc[0][1:]], axis=0)
    ktoe = jnp.stack([z[t_len - 1 - tp:2 * t_len - 1 - tp] for tp in range(t_len)])
    ktoe = ktoe.transpose(2, 0, 3, 1, 4)
    w_intra = ktoe[:, :, None, :, :, :] * mask_gh[None, None, :, None, None, :]
    parts = [w_intra.reshape(SSM_TILES, t_len * V7X_LANES, t_len * V7X_LANES)]
    for d in range(2):
        for comp in ef[d]:
            c4 = comp.reshape(t_len, SSM_TILES, lg, p, hg).transpose(1, 0, 4, 2, 3).reshape(
                SSM_TILES, t_len, hg, lg * p)
            w = c4[:, :, None, :, :] * mask_gp[None, None, :, None, :]
            parts.append(w.reshape(SSM_TILES, t_len * V7X_LANES, lg * p))
    w1 = jnp.concatenate(parts, axis=-1)
    rows = []
    for d in range(2):
        m2r, m2i = mf[d]
        for comp in (m2r, -m2i):
            c4 = comp.reshape(t_len, SSM_TILES, lg, hg, p).transpose(1, 4, 0, 2, 3).reshape(
                SSM_TILES, p, t_len, V7X_LANES)
            w = c4[:, None, :, :, :] * mask_gh[None, :, None, None, :]
            rows.append(w.reshape(SSM_TILES, lg * p, t_len * V7X_LANES))
    wc = jnp.concatenate(rows, axis=1)
    decay = jnp.stack([x.reshape(SSM_TILES, lg * p) for d in range(2) for x in dec[d]], axis=1)
    return w1.astype(BF16), wc.astype(BF16), decay


def _ssm_kernel(uc_ref, u_ref, w1_ref, wc_ref, dec_ref, dsk_ref, yc_ref, y_ref,
                uf_ref, lhs_ref, yi_ref, es_ref, *, n_ctx, seq):
    t_len = SSM_CHUNK
    nc, lc = n_ctx // t_len, seq // t_len
    nj = nc + lc
    kdim = t_len * V7X_LANES
    sw = SSM_LANE_GROUPS * SSM_STATE
    uf_ref[0:n_ctx, :] = uc_ref[...].astype(F32)
    uf_ref[n_ctx:n_ctx + seq, :] = u_ref[...].astype(F32)
    for t in range(t_len):
        lhs_ref[:, t * V7X_LANES:(t + 1) * V7X_LANES] = uf_ref[pl.ds(t, nj, stride=t_len), :]
    lhs = lhs_ref[...].astype(BF16)
    yi_ref[...] = jnp.dot(lhs, w1_ref[0, :, 0:kdim], preferred_element_type=F32)
    es_ref[:, 0:2 * sw] = jnp.dot(lhs, w1_ref[0, :, kdim:kdim + 2 * sw], preferred_element_type=F32)
    es_ref[:, 2 * sw:4 * sw] = jnp.dot(lhs, w1_ref[0, :, kdim + 2 * sw:kdim + 4 * sw], preferred_element_type=F32)

    dec = dec_ref[0]
    afr, afi, abr, abi = dec[0:1], dec[1:2], dec[2:3], dec[3:4]

    def step(k, st):
        sfr, sfi, sbr, sbi = st
        jb = jnp.where(k < nc, nc - 1 - k, nj - 1 - (k - nc))
        efr = es_ref[pl.ds(k, 1), 0:sw]
        efi = es_ref[pl.ds(k, 1), sw:2 * sw]
        ebr = es_ref[pl.ds(jb, 1), 2 * sw:3 * sw]
        ebi = es_ref[pl.ds(jb, 1), 3 * sw:4 * sw]
        es_ref[pl.ds(k, 1), 0:sw] = sfr
        es_ref[pl.ds(k, 1), sw:2 * sw] = sfi
        es_ref[pl.ds(jb, 1), 2 * sw:3 * sw] = sbr
        es_ref[pl.ds(jb, 1), 3 * sw:4 * sw] = sbi
        nfr, nfi = _cmul(afr, afi, sfr, sfi)
        nbr, nbi = _cmul(abr, abi, sbr, sbi)
        return nfr + efr, nfi + efi, nbr + ebr, nbi + ebi

    z = jnp.zeros((1, sw), F32)
    lax.fori_loop(0, nj, step, (z, z, z, z))

    y = (yi_ref[...] + jnp.dot(es_ref[...].astype(BF16), wc_ref[0], preferred_element_type=F32)
         + lhs_ref[...] * dsk_ref[0])
    yi_ref[...] = y
    for t in range(t_len):
        uf_ref[pl.ds(t, nj, stride=t_len), :] = yi_ref[:, t * V7X_LANES:(t + 1) * V7X_LANES]
    yc_ref[...] = uf_ref[0:n_ctx, :]
    y_ref[...] = uf_ref[n_ctx:n_ctx + seq, :]


def _ssm_call(proj, projc, w1, wc, decay, dsk_tiled, batch, seq, n_ctx):
    t_len = SSM_CHUNK
    nj = (n_ctx + seq) // t_len
    kdim = t_len * V7X_LANES
    sw4 = 4 * SSM_LANE_GROUPS * SSM_STATE
    col0 = COL_SSM_U * BRANCH // V7X_LANES
    return pl.pallas_call(
        functools.partial(_ssm_kernel, n_ctx=n_ctx, seq=seq),
        grid=(SSM_TILES, batch),
        in_specs=[
            pl.BlockSpec((n_ctx, V7X_LANES), lambda s, b: (b, col0 + s)),
            pl.BlockSpec((seq, V7X_LANES), lambda s, b: (b, col0 + s)),
            pl.BlockSpec((1, kdim, kdim + sw4), lambda s, b: (s, 0, 0)),
            pl.BlockSpec((1, sw4, kdim), lambda s, b: (s, 0, 0)),
            pl.BlockSpec((1, 4, sw4 // 4), lambda s, b: (s, 0, 0)),
            pl.BlockSpec((1, 1, kdim), lambda s, b: (s, 0, 0)),
        ],
        out_specs=[
            pl.BlockSpec((n_ctx, V7X_LANES), lambda s, b: (b, s)),
            pl.BlockSpec((seq, V7X_LANES), lambda s, b: (b, s)),
        ],
        out_shape=[
            jax.ShapeDtypeStruct((batch * n_ctx, BRANCH), F32),
            jax.ShapeDtypeStruct((batch * seq, BRANCH), F32),
        ],
        scratch_shapes=[
            pltpu.VMEM((n_ctx + seq, V7X_LANES), F32),
            pltpu.VMEM((nj, kdim), F32),
            pltpu.VMEM((nj, kdim), F32),
            pltpu.VMEM((nj, sw4), F32),
        ],
        compiler_params=_cparams(("arbitrary", "arbitrary")),
        name="ssm",
    )(projc, proj, w1, wc, decay, dsk_tiled)


def _merge_kernel(x_ref, gate_ref, gpost_ref,
                  naz_ref, pu_ref, pz_ref, cx_ref, cb_ref, cc_ref, cz_ref, sz_ref,
                  lg0_ref, lg1_ref, lg2_ref, lg3_ref,
                  pu_prev_ref, pu_next_ref, cx_prev_ref, cx_next_ref, cc_prev_ref, cc_next_ref,
                  ona_ref, yssm_ref,
                  poolw_ref, pscale_ref, convw_ref, gluw_ref, bgate_ref, wbr_ref, wo_ref,
                  o_ref, pad_ref, *, tm, rows_per_seq):
    i = pl.program_id(0)
    tiles_per_seq = rows_per_seq // tm
    ti = i % tiles_per_seq
    has_prev = ti > 0
    has_next = ti < tiles_per_seq - 1
    t0 = ti * tm
    pos = t0 + lax.broadcasted_iota(jnp.int32, (tm, 1), 0)

    pad_ref[0:HALO, :] = jnp.where(has_prev, pu_prev_ref[...].astype(F32), 0.0)
    pad_ref[HALO:HALO + tm, :] = pu_ref[...].astype(F32)
    pad_ref[HALO + tm:2 * HALO + tm, :] = jnp.where(has_next, pu_next_ref[...].astype(F32), 0.0)
    mixed = []
    for gi, w in enumerate(POOL_WINDOWS):
        cs = slice(gi * POOL_GROUP_DIM, (gi + 1) * POOL_GROUP_DIM)
        acc = None
        for dlt in range(-(w // 2), w - w // 2):
            v = pad_ref[HALO + dlt:HALO + dlt + tm, cs]
            acc = v if acc is None else acc + v
        lo = jnp.maximum(pos - w // 2, 0)
        hi = jnp.minimum(pos + w - w // 2, rows_per_seq)
        cnt = (hi - lo).astype(F32)
        pooled = acc / cnt - pad_ref[HALO:HALO + tm, cs]
        mixed.append(jnp.dot(pooled.astype(BF16), poolw_ref[gi], preferred_element_type=F32))
    o_pool = jnp.concatenate(mixed, axis=-1) * pscale_ref[...]

    pad_ref[0:HALO, :] = jnp.where(
        has_prev, cc_prev_ref[...].astype(F32) * cx_prev_ref[...].astype(F32), 0.0)
    pad_ref[HALO:HALO + tm, :] = cc_ref[...].astype(F32) * cx_ref[...].astype(F32)
    pad_ref[HALO + tm:2 * HALO + tm, :] = jnp.where(
        has_next, cc_next_ref[...].astype(F32) * cx_next_ref[...].astype(F32), 0.0)
    cw = convw_ref[...]
    conv = (pad_ref[HALO - 1:HALO - 1 + tm, :] * cw[0:1] + pad_ref[HALO:HALO + tm, :] * cw[1:2]
            + pad_ref[HALO + 1:HALO + 1 + tm, :] * cw[2:3])
    o_conv = cb_ref[...].astype(F32) * conv

    g = jax.nn.gelu(yssm_ref[...]).astype(BF16)
    gg = jnp.dot(g, gluw_ref[...], preferred_element_type=F32)
    o_ssm = gg[:, 0:BRANCH] * jax.nn.sigmoid(gg[:, BRANCH:2 * BRANCH])

    outs = (ona_ref[...].astype(F32), o_pool, o_conv, o_ssm)
    zs = (naz_ref, pz_ref, cz_ref, sz_ref)
    lgs = (lg0_ref, lg1_ref, lg2_ref, lg3_ref)
    merged = None
    for bi in range(N_BRANCHES):
        z = zs[bi][...].astype(F32)
        a = (outs[bi] * (z * jax.nn.sigmoid(z))).astype(BF16)
        br = jnp.dot(a, wbr_ref[bi * BRANCH:(bi + 1) * BRANCH, :], preferred_element_type=F32)
        gt = jax.nn.sigmoid(lgs[bi][...].astype(F32) + bgate_ref[:, bi * D_MODEL:(bi + 1) * D_MODEL])
        term = gt * br
        merged = term if merged is None else merged + term
    y = jnp.dot(merged.astype(BF16), wo_ref[...], preferred_element_type=F32)
    ms = jnp.mean(y * y, axis=-1, keepdims=True)
    yn = y * lax.rsqrt(ms + RMS_EPS) * gpost_ref[...]
    o_ref[...] = x_ref[...] + gate_ref[0] * yn


def _merge_call(x2, gate, g_post, proj, o_na, y_ssm, pool_w, pool_scale, conv_w, glu_w, b_gate,
                w_br, w_o, rows_per_seq, tm):
    r, d = x2.shape
    tiles_per_seq = rows_per_seq // tm
    hb = tm // HALO
    n_halo_blocks = r // HALO

    def col(c):
        return pl.BlockSpec((tm, BRANCH), lambda i: (i, c))

    def lg(c):
        return pl.BlockSpec((tm, D_MODEL), lambda i: (i, COL_MERGE * BRANCH // D_MODEL + c))

    def prev(c):
        return pl.BlockSpec((HALO, BRANCH), lambda i: (jnp.maximum(i * hb - 1, 0), c))

    def nxt(c):
        return pl.BlockSpec((HALO, BRANCH), lambda i: (jnp.minimum((i + 1) * hb, n_halo_blocks - 1), c))

    in_specs = [
        pl.BlockSpec((tm, d), lambda i: (i, 0)),
        pl.BlockSpec((1, 1, d), lambda i: (i // tiles_per_seq, 0, 0)),
        pl.BlockSpec((1, d), lambda i: (0, 0)),
        col(COL_NA_Z), col(COL_POOL_U), col(COL_POOL_Z), col(COL_CONV_X), col(COL_CONV_B),
        col(COL_CONV_C), col(COL_CONV_Z), col(COL_SSM_Z),
        lg(0), lg(1), lg(2), lg(3),
        prev(COL_POOL_U), nxt(COL_POOL_U), prev(COL_CONV_X), nxt(COL_CONV_X),
        prev(COL_CONV_C), nxt(COL_CONV_C),
        pl.BlockSpec((tm, BRANCH), lambda i: (i, 0)),
        pl.BlockSpec((tm, BRANCH), lambda i: (i, 0)),
        _const_spec(pool_w.shape), _const_spec(pool_scale.shape), _const_spec(conv_w.shape),
        _const_spec(glu_w.shape), _const_spec(b_gate.shape), _const_spec(w_br.shape),
        _const_spec(w_o.shape),
    ]
    args = [x2, gate, g_post] + [proj] * 8 + [proj] * 4 + [proj] * 6 + [
        o_na, y_ssm, pool_w, pool_scale, conv_w, glu_w, b_gate, w_br, w_o]
    return pl.pallas_call(
        functools.partial(_merge_kernel, tm=tm, rows_per_seq=rows_per_seq),
        grid=(r // tm,),
        in_specs=in_specs,
        out_specs=pl.BlockSpec((tm, d), lambda i: (i, 0)),
        out_shape=jax.ShapeDtypeStruct((r, d), F32),
        scratch_shapes=[pltpu.VMEM((tm + 2 * HALO, BRANCH), F32)],
        compiler_params=_cparams(("arbitrary",)),
        name="merge",
    )(*args)


def _tile_rows(rows, pref):
    t = min(rows, pref)
    assert rows % t == 0
    return t


def kernel(x, c, ctx, c_ctx, w_mod, b_mod, g_pre, g_post, w_in, b_gate, na_rpb, pool_w, pool_scale,
           conv_w, ssm_a_re, ssm_a_im, ssm_log_dt, ssm_b_re, ssm_b_im, ssm_c_re, ssm_c_im, ssm_d,
           glu_w, w_br, w_o):
    batch, seq, d = x.shape
    n_ctx = ctx.shape[1]
    depth = w_mod.shape[0]
    assert d == D_MODEL and seq % (GRID_W * NA_WIN_ROWS) == 0 and batch + 1 <= 8
    assert n_ctx % HALO == 0 and w_in.shape[-1] == IN_TOTAL

    c8 = jnp.concatenate([c, c_ctx[None], jnp.zeros((7 - batch, d), F32)], axis=0)
    mod = _mod_call(c8, w_mod, b_mod)

    x2 = x.reshape(batch * seq, d)
    xc2 = ctx.reshape(batch * n_ctx, d)
    tm_in = _tile_rows(seq, 1024)
    tm_in_c = _tile_rows(batch * n_ctx, 1024)
    tm_mg = _tile_rows(seq, 256)
    tm_mg_c = _tile_rows(n_ctx, 256)

    bias_tbls = jax.vmap(_na_bias_table)(na_rpb)
    w1s, wcs, decays = jax.vmap(_ssm_weights)(ssm_a_re, ssm_a_im, ssm_log_dt, ssm_b_re, ssm_b_im,
                                              ssm_c_re, ssm_c_im)

    for i in range(depth):
        with_ctx_out = i < depth - 1
        shift, scale, gate = (mod[i, :, k * d:(k + 1) * d] for k in range(3))
        lat = lambda v: v[:batch].reshape(batch, 1, d)
        cx = lambda v: v[batch:batch + 1].reshape(1, 1, d)
        w_bf = w_in[i].astype(BF16)
        gp = g_pre[i].reshape(1, d)
        gq = g_post[i].reshape(1, d)

        proj = _inproj_call(x2, lat(scale), lat(shift), gp, w_bf, seq, tm_in, 1024)
        projc = _inproj_call(xc2, cx(scale), cx(shift), gp, w_bf, batch * n_ctx, tm_in_c, 1024)

        o_na = _na_call(proj, projc, bias_tbls[i], batch, seq, n_ctx)

        dsk = jnp.tile(ssm_d[i].astype(F32).reshape(SSM_TILES, 1, V7X_LANES), (1, 1, SSM_CHUNK))
        yc_ssm, y_ssm = _ssm_call(proj, projc, w1s[i], wcs[i], decays[i], dsk, batch, seq, n_ctx)

        mw = (pool_w[i].astype(BF16), pool_scale[i].reshape(1, BRANCH), conv_w[i],
              glu_w[i].astype(BF16), b_gate[i].reshape(1, N_BRANCHES * d),
              w_br[i].astype(BF16), w_o[i].astype(BF16))
        x2 = _merge_call(x2, lat(gate), gq, proj, o_na, y_ssm, *mw, seq, tm_mg)
        if with_ctx_out:
            oc_na = _ctx_attn_call(projc, batch, n_ctx)
            gate_c = jnp.broadcast_to(cx(gate), (batch, 1, d))
            xc2 = _merge_call(xc2, gate_c, gq, projc, oc_na, yc_ssm, *mw, n_ctx, tm_mg_c)
    return x2.reshape(batch, seq, d)
```

```python
import functools

import jax
import jax.numpy as jnp
import numpy as np
from jax import lax
from jax.experimental import pallas as pl
from jax.experimental.pallas import tpu as pltpu

F32 = jnp.float32
BF16 = jnp.bfloat16

D_MODEL = 2048
GRID_W = 64
RMS_EPS = 1e-6
NEG_INF = -1e30
BRANCH = 512
N_BRANCHES = 4
NA_HEADS = 8
NA_HEAD_DIM = 64
NA_WIN_ROWS = 8
NA_WIN_COLS = 16
NA_GROUP_ROWS = 4
NA_KEY_ROWS = 12
POOL_WINDOWS = (2, 4, 8, 16)
POOL_GROUP_DIM = 128
SSM_GROUPS = 32
SSM_GROUP_DIM = 16
SSM_STATE = 64
IN_TOTAL = 12 * BRANCH + N_BRANCHES * D_MODEL
COL_Q, COL_K, COL_V, COL_NA_Z = 0, 1, 2, 3
COL_POOL_U, COL_POOL_Z = 4, 5
COL_CONV_X, COL_CONV_B, COL_CONV_C, COL_CONV_Z = 6, 7, 8, 9
COL_SSM_U, COL_SSM_Z = 10, 11
COL_MERGE = 12

V7X_LANES = 128
SSM_CHUNK = 8
SSM_LANE_GROUPS = V7X_LANES // SSM_GROUP_DIM
SSM_TILES = BRANCH // V7X_LANES
HALO = 16
VMEM_LIMIT = 56 * 1024 * 1024


def _cparams(sem):
    return pltpu.CompilerParams(dimension_semantics=sem, vmem_limit_bytes=VMEM_LIMIT)


def _const_spec(shape):
    nd = len(shape)
    return pl.BlockSpec(shape, lambda *_: (0,) * nd, pipeline_mode=pl.Buffered(1))


def _layer_spec(stacked_shape, layer):
    nd = len(stacked_shape) - 1
    return pl.BlockSpec((None,) + tuple(stacked_shape[1:]), lambda *_: (layer,) + (0,) * nd,
                        pipeline_mode=pl.Buffered(1))


def _mod_kernel(c_ref, w_ref, b_ref, o_ref):
    c = c_ref[...]
    a = (c * jax.nn.sigmoid(c)).astype(BF16)
    o_ref[0] = jnp.dot(a, w_ref[0].astype(BF16), preferred_element_type=F32) + b_ref[0]


def _mod_call(c8, w_mod, b_mod):
    depth, d, n3 = w_mod.shape
    tn = 512
    return pl.pallas_call(
        _mod_kernel,
        grid=(depth, n3 // tn),
        in_specs=[
            pl.BlockSpec((8, d), lambda l, j: (0, 0)),
            pl.BlockSpec((1, d, tn), lambda l, j: (l, 0, j)),
            pl.BlockSpec((1, 1, tn), lambda l, j: (l, 0, j)),
        ],
        out_specs=pl.BlockSpec((1, 8, tn), lambda l, j: (l, 0, j)),
        out_shape=jax.ShapeDtypeStruct((depth, 8, n3), F32),
        compiler_params=_cparams(("arbitrary", "arbitrary")),
        name="mod",
    )(c8, w_mod, b_mod.reshape(depth, 1, n3))


def _inproj_kernel(x_ref, scale_ref, shift_ref, g_ref, w_ref, o_ref, h_ref):
    @pl.when(pl.program_id(1) == 0)
    def _():
        x = x_ref[...]
        ms = jnp.mean(x * x, axis=-1, keepdims=True)
        y = x * lax.rsqrt(ms + RMS_EPS) * g_ref[...]
        h_ref[...] = (y * (1.0 + scale_ref[0]) + shift_ref[0]).astype(BF16)

    o_ref[...] = jnp.dot(h_ref[...], w_ref[...], preferred_element_type=F32).astype(o_ref.dtype)


def _inproj_call(x2, scale, shift, g_pre, w_bf, layer, rows_per_seq, tm, tn):
    r, d = x2.shape
    n = w_bf.shape[2]
    tiles_per_seq = rows_per_seq // tm
    return pl.pallas_call(
        _inproj_kernel,
        grid=(r // tm, n // tn),
        in_specs=[
            pl.BlockSpec((tm, d), lambda i, j: (i, 0)),
            pl.BlockSpec((1, 1, d), lambda i, j: (i // tiles_per_seq, 0, 0)),
            pl.BlockSpec((1, 1, d), lambda i, j: (i // tiles_per_seq, 0, 0)),
            pl.BlockSpec((1, d), lambda i, j: (0, 0)),
            pl.BlockSpec((None, d, tn), lambda i, j: (layer, 0, j)),
        ],
        out_specs=pl.BlockSpec((tm, tn), lambda i, j: (i, j)),
        out_shape=jax.ShapeDtypeStruct((r, n), BF16),
        scratch_shapes=[pltpu.VMEM((tm, d), BF16)],
        compiler_params=_cparams(("arbitrary", "arbitrary")),
        name="inproj",
    )(x2, scale, shift, g_pre, w_bf)


def _head_mask(rows):
    lane = lax.broadcasted_iota(jnp.int32, (rows, V7X_LANES), 1)
    return lane < NA_HEAD_DIM


def _na_kernel(q_ref, k_ref, v_ref, kc_ref, vc_ref, bias_ref, o_ref, *, grid_rows):
    r0 = pl.program_id(1) * NA_GROUP_ROWS
    nq = NA_GROUP_ROWS * GRID_W
    first_half = _head_mask(nq)
    scale = NA_HEAD_DIM ** -0.5
    nt = (((1,), (1,)), ((), ()))
    gs = jnp.clip(r0 - NA_WIN_ROWS // 2, 0, grid_rows - NA_KEY_ROWS)
    cls = jnp.where(r0 == 0, 0, jnp.where(r0 == grid_rows - NA_GROUP_ROWS, 2, 1))
    k0 = pl.multiple_of(gs * GRID_W, GRID_W)
    for hp in range(NA_HEADS // 2):
        cs = slice(hp * V7X_LANES, (hp + 1) * V7X_LANES)
        qp = q_ref[:, cs] * jnp.asarray(scale, BF16)
        kp = k_ref[pl.ds(k0, NA_KEY_ROWS * GRID_W), cs]
        vp = v_ref[pl.ds(k0, NA_KEY_ROWS * GRID_W), cs]
        kcp = kc_ref[:, cs]
        vcp = vc_ref[:, cs]
        zero = jnp.zeros_like(qp)
        q2 = jnp.concatenate([jnp.where(first_half, qp, zero), jnp.where(first_half, zero, qp)], axis=0)
        s_b = lax.dot_general(q2, kp, nt, preferred_element_type=F32) + bias_ref[cls, hp]
        s_c = lax.dot_general(q2, kcp, nt, preferred_element_type=F32)
        m = jnp.maximum(jnp.max(s_b, axis=-1, keepdims=True), jnp.max(s_c, axis=-1, keepdims=True))
        p_b = jnp.exp(s_b - m)
        p_c = jnp.exp(s_c - m)
        denom = jnp.sum(p_b, axis=-1, keepdims=True) + jnp.sum(p_c, axis=-1, keepdims=True)
        o = (jnp.dot(p_b.astype(BF16), vp, preferred_element_type=F32)
             + jnp.dot(p_c.astype(BF16), vcp, preferred_element_type=F32)) / denom
        o_ref[:, cs] = jnp.where(first_half, o[0:nq], o[nq:]).astype(o_ref.dtype)


def _na_call(proj, projc, bias_tbls, layer, batch, seq, n_ctx):
    grid_rows = seq // GRID_W
    assert grid_rows % NA_GROUP_ROWS == 0 and grid_rows >= NA_KEY_ROWS
    tq = NA_GROUP_ROWS * GRID_W
    qb = seq // tq
    return pl.pallas_call(
        functools.partial(_na_kernel, grid_rows=grid_rows),
        grid=(batch, qb),
        in_specs=[
            pl.BlockSpec((tq, BRANCH), lambda b, r: (b * qb + r, COL_Q)),
            pl.BlockSpec((seq, BRANCH), lambda b, r: (b, COL_K)),
            pl.BlockSpec((seq, BRANCH), lambda b, r: (b, COL_V)),
            pl.BlockSpec((n_ctx, BRANCH), lambda b, r: (b, COL_K)),
            pl.BlockSpec((n_ctx, BRANCH), lambda b, r: (b, COL_V)),
            _layer_spec(bias_tbls.shape, layer),
        ],
        out_specs=pl.BlockSpec((tq, BRANCH), lambda b, r: (b * qb + r, 0)),
        out_shape=jax.ShapeDtypeStruct((batch * seq, BRANCH), BF16),
        compiler_params=_cparams(("arbitrary", "arbitrary")),
        name="na_attn",
    )(proj, proj, proj, projc, projc, bias_tbls)


def _ctx_attn_kernel(q_ref, k_ref, v_ref, o_ref):
    n = q_ref.shape[0]
    first_half = _head_mask(n)
    scale = NA_HEAD_DIM ** -0.5
    nt = (((1,), (1,)), ((), ()))
    for hp in range(NA_HEADS // 2):
        cs = slice(hp * V7X_LANES, (hp + 1) * V7X_LANES)
        qp = q_ref[:, cs] * jnp.asarray(scale, BF16)
        kp = k_ref[:, cs]
        vp = v_ref[:, cs]
        zero = jnp.zeros_like(qp)
        q2 = jnp.concatenate([jnp.where(first_half, qp, zero), jnp.where(first_half, zero, qp)], axis=0)
        s = lax.dot_general(q2, kp, nt, preferred_element_type=F32)
        m = jnp.max(s, axis=-1, keepdims=True)
        p = jnp.exp(s - m)
        denom = jnp.sum(p, axis=-1, keepdims=True)
        o = jnp.dot(p.astype(BF16), vp, preferred_element_type=F32) / denom
        o_ref[:, cs] = jnp.where(first_half, o[0:n], o[n:]).astype(o_ref.dtype)


def _ctx_attn_call(projc, batch, n_ctx):
    return pl.pallas_call(
        _ctx_attn_kernel,
        grid=(batch,),
        in_specs=[
            pl.BlockSpec((n_ctx, BRANCH), lambda b: (b, COL_Q)),
            pl.BlockSpec((n_ctx, BRANCH), lambda b: (b, COL_K)),
            pl.BlockSpec((n_ctx, BRANCH), lambda b: (b, COL_V)),
        ],
        out_specs=pl.BlockSpec((n_ctx, BRANCH), lambda b: (b, 0)),
        out_shape=jax.ShapeDtypeStruct((batch * n_ctx, BRANCH), BF16),
        compiler_params=_cparams(("arbitrary",)),
        name="ctx_attn",
    )(projc, projc, projc)


def _na_bias_table(rpb):
    col = np.arange(GRID_W)
    col_start = np.clip(col - NA_WIN_COLS // 2, 0, GRID_W - NA_WIN_COLS)
    in_win = (col[None, :] >= col_start[:, None]) & (col[None, :] < col_start[:, None] + NA_WIN_COLS)
    dcol = np.clip(col[None, :] - col[:, None] + (NA_WIN_COLS - 1), 0, 2 * NA_WIN_COLS - 2)
    onehot = (dcol[..., None] == np.arange(2 * NA_WIN_COLS - 1)).astype(np.float32)
    t = jnp.einsum('hdc,qwc->hqdw', rpb.astype(F32), onehot, precision=lax.Precision.HIGHEST)
    t = jnp.where(in_win[None, :, None, :], t, NEG_INF)
    t = t.reshape(NA_HEADS // 2, 2, GRID_W, 2 * NA_WIN_ROWS - 1, GRID_W)
    classes = []
    for c in range(3):
        slabs = []
        for i in range(NA_GROUP_ROWS):
            lo = (0, i, NA_KEY_ROWS - NA_WIN_ROWS)[c]
            off = (NA_WIN_ROWS - 1 - i, NA_WIN_ROWS // 2 - 1 - i, NA_WIN_ROWS // 2 - 1 - lo - i)[c]
            hi = lo + NA_WIN_ROWS
            slabs.append(jnp.pad(t[:, :, :, lo + off:hi + off],
                                 ((0, 0), (0, 0), (0, 0), (lo, NA_KEY_ROWS - hi), (0, 0)),
                                 constant_values=NEG_INF))
        classes.append(jnp.stack(slabs, axis=2))
    b = jnp.stack(classes)
    return b.reshape(3, NA_HEADS // 2, 2 * NA_GROUP_ROWS * GRID_W, NA_KEY_ROWS * GRID_W)


def _cmul(ar, ai, br, bi):
    return ar * br - ai * bi, ar * bi + ai * br


def _ssm_weights(a_re, a_im, log_dt, b_re, b_im, c_re, c_im):
    t_len = SSM_CHUNK
    p, hg = SSM_STATE, SSM_GROUP_DIM
    lg = SSM_LANE_GROUPS
    kf, ef, mf, dec = [], [], [], []
    for d in range(2):
        ar, ai = a_re[d].astype(F32), a_im[d].astype(F32)
        dt = jnp.exp(log_dt[d].astype(F32))[:, None]
        mag = jnp.exp(ar * dt)
        abr, abi = mag * jnp.cos(ai * dt), mag * jnp.sin(ai * dt)
        den = ar * ar + ai * ai
        nr, ni = abr - 1.0, abi
        fr = (nr * ar + ni * ai) / den
        fi = (ni * ar - nr * ai) / den
        bbr, bbi = _cmul(fr[..., None], fi[..., None], b_re[d].astype(F32), b_im[d].astype(F32))
        pws = [(jnp.ones_like(abr), jnp.zeros_like(abr))]
        for _ in range(t_len):
            pws.append(_cmul(pws[-1][0], pws[-1][1], abr, abi))
        pr = jnp.stack([x[0] for x in pws])
        pi = jnp.stack([x[1] for x in pws])
        cr, ci = c_re[d].astype(F32), c_im[d].astype(F32)
        mr, mi = _cmul(cr[None], ci[None], pr[:t_len, :, None, :], pi[:t_len, :, None, :])
        kf.append(jnp.einsum('tghp,gpk->tghk', mr, bbr) - jnp.einsum('tghp,gpk->tghk', mi, bbi))
        e_pw = [pws[t_len - 1 - t] if d == 0 else pws[t] for t in range(t_len)]
        per, pei = jnp.stack([x[0] for x in e_pw]), jnp.stack([x[1] for x in e_pw])
        er, ei = _cmul(per[..., None], pei[..., None], bbr[None], bbi[None])
        ef.append((er, ei))
        c_pw = [pws[t + 1] if d == 0 else pws[t_len - t] for t in range(t_len)]
        pcr, pci = jnp.stack([x[0] for x in c_pw]), jnp.stack([x[1] for x in c_pw])
        m2r, m2i = _cmul(cr[None], ci[None], pcr[:, :, None, :], pci[:, :, None, :])
        mf.append((m2r, m2i))
        dec.append((pr[t_len], pi[t_len]))
    kc = [k.reshape(t_len, SSM_TILES, lg, hg, hg).transpose(0, 1, 4, 2, 3).reshape(
        t_len, SSM_TILES, hg, V7X_LANES) for k in kf]
    z = ([kc[1][t_len - 1 - i] for i in range(t_len - 1)] + [kc[0][0] + kc[1][0]]
         + [kc[0][i] for i in range(1, t_len)])
    ktoe = jnp.stack([jnp.stack(z[t_len - 1 - tp:2 * t_len - 1 - tp]) for tp in range(t_len)])
    parts = [ktoe.transpose(2, 0, 3, 1, 4).reshape(SSM_TILES, t_len, hg, t_len * V7X_LANES)]
    for d in range(2):
        for comp in ef[d]:
            parts.append(comp.reshape(t_len, SSM_TILES, lg, p, hg).transpose(1, 0, 4, 2, 3).reshape(
                SSM_TILES, t_len, hg, lg * p))
    c1 = jnp.concatenate(parts, axis=-1)
    rows = []
    for d in range(2):
        m2r, m2i = mf[d]
        for comp in (m2r, -m2i):
            rows.append(comp.reshape(t_len, SSM_TILES, lg, hg, p).transpose(1, 4, 0, 2, 3).reshape(
                SSM_TILES, p, t_len * V7X_LANES))
    cc = jnp.stack(rows, axis=1)
    decay = jnp.stack([x.reshape(SSM_TILES, SSM_TILES, V7X_LANES) for d in range(2) for x in dec[d]], axis=1)
    return c1, cc, decay


def _ssm_kernel(uc_ref, u_ref, c1_ref, cc_ref, dec_ref, dsk_ref, yc_ref, y_ref,
                w1_ref, wc_ref, uf_ref, lhs_ref, yi_ref, es_ref, *, n_ctx, seq):
    t_len = SSM_CHUNK
    nc, lc = n_ctx // t_len, seq // t_len
    nj = nc + lc
    kdim = t_len * V7X_LANES
    sw = SSM_LANE_GROUPS * SSM_STATE
    ntile = sw // V7X_LANES
    hg = SSM_GROUP_DIM

    @pl.when(pl.program_id(1) == 0)
    def _():
        lane1 = lax.broadcasted_iota(jnp.int32, (1, kdim + 4 * sw), 1)
        grp1 = jnp.where(lane1 < kdim, (lane1 % V7X_LANES) // hg, ((lane1 - kdim) % sw) // SSM_STATE)
        lanec = lax.broadcasted_iota(jnp.int32, (1, kdim), 1)
        grpc = (lanec % V7X_LANES) // hg
        for t in range(t_len):
            row = c1_ref[0, t]
            for g in range(SSM_LANE_GROUPS):
                r0 = (t * SSM_LANE_GROUPS + g) * hg
                w1_ref[r0:r0 + hg, :] = jnp.where(grp1 == g, row, 0.0).astype(BF16)
        for part in range(4):
            blk = cc_ref[0, part]
            for g in range(SSM_LANE_GROUPS):
                r0 = part * sw + g * SSM_STATE
                wc_ref[r0:r0 + SSM_STATE, :] = jnp.where(grpc == g, blk, 0.0).astype(BF16)

    uf_ref[0:n_ctx, :] = uc_ref[...].astype(F32)
    uf_ref[n_ctx:n_ctx + seq, :] = u_ref[...].astype(F32)
    for t in range(t_len):
        lhs_ref[:, t * V7X_LANES:(t + 1) * V7X_LANES] = uf_ref[pl.ds(t, nj, stride=t_len), :]
    lhs = lhs_ref[...].astype(BF16)
    yi_ref[...] = jnp.dot(lhs, w1_ref[:, 0:kdim], preferred_element_type=F32) + lhs_ref[...] * dsk_ref[0]
    for c in range(4):
        e = jnp.dot(lhs, w1_ref[:, kdim + c * sw:kdim + (c + 1) * sw], preferred_element_type=F32)
        for k in range(ntile):
            es_ref[c, pl.ds(k, nj, stride=ntile), :] = e[:, k * V7X_LANES:(k + 1) * V7X_LANES]

    afr, afi, abr, abi = dec_ref[0, 0], dec_ref[0, 1], dec_ref[0, 2], dec_ref[0, 3]

    def step(k, st):
        sfr, sfi, sbr, sbi = st
        jb = jnp.where(k < nc, nc - 1 - k, nj - 1 - (k - nc))
        rf = pl.ds(pl.multiple_of(k * ntile, ntile), ntile)
        rb = pl.ds(pl.multiple_of(jb * ntile, ntile), ntile)
        efr, efi = es_ref[0, rf, :], es_ref[1, rf, :]
        ebr, ebi = es_ref[2, rb, :], es_ref[3, rb, :]
        es_ref[0, rf, :] = sfr
        es_ref[1, rf, :] = sfi
        es_ref[2, rb, :] = sbr
        es_ref[3, rb, :] = sbi
        nfr, nfi = _cmul(afr, afi, sfr, sfi)
        nbr, nbi = _cmul(abr, abi, sbr, sbi)
        return nfr + efr, nfi + efi, nbr + ebr, nbi + ebi

    z = jnp.zeros((ntile, V7X_LANES), F32)
    lax.fori_loop(0, nj, step, (z, z, z, z))

    carried = jnp.concatenate(
        [es_ref[c, pl.ds(k, nj, stride=ntile), :] for c in range(4) for k in range(ntile)], axis=1)
    yi_ref[...] += jnp.dot(carried.astype(BF16), wc_ref[...], preferred_element_type=F32)
    for t in range(t_len):
        uf_ref[pl.ds(t, nj, stride=t_len), :] = yi_ref[:, t * V7X_LANES:(t + 1) * V7X_LANES]
    yc_ref[...] = uf_ref[0:n_ctx, :]
    y_ref[...] = uf_ref[n_ctx:n_ctx + seq, :]


def _ssm_call(proj, projc, c1, cc, decay, dsk_tiled, layer, batch, seq, n_ctx):
    t_len = SSM_CHUNK
    nj = (n_ctx + seq) // t_len
    kdim = t_len * V7X_LANES
    sw4 = 4 * SSM_LANE_GROUPS * SSM_STATE
    ntile = sw4 // 4 // V7X_LANES
    col0 = COL_SSM_U * BRANCH // V7X_LANES
    return pl.pallas_call(
        functools.partial(_ssm_kernel, n_ctx=n_ctx, seq=seq),
        grid=(SSM_TILES, batch),
        in_specs=[
            pl.BlockSpec((n_ctx, V7X_LANES), lambda s, b: (b, col0 + s)),
            pl.BlockSpec((seq, V7X_LANES), lambda s, b: (b, col0 + s)),
            pl.BlockSpec((None, 1, t_len, SSM_GROUP_DIM, kdim + sw4), lambda s, b: (layer, s, 0, 0, 0)),
            pl.BlockSpec((None, 1, 4, SSM_STATE, kdim), lambda s, b: (layer, s, 0, 0, 0)),
            pl.BlockSpec((None, 1, 4, ntile, V7X_LANES), lambda s, b: (layer, s, 0, 0, 0)),
            pl.BlockSpec((None, 1, 1, kdim), lambda s, b: (layer, s, 0, 0)),
        ],
        out_specs=[
            pl.BlockSpec((n_ctx, V7X_LANES), lambda s, b: (b, s)),
            pl.BlockSpec((seq, V7X_LANES), lambda s, b: (b, s)),
        ],
        out_shape=[
            jax.ShapeDtypeStruct((batch * n_ctx, BRANCH), F32),
            jax.ShapeDtypeStruct((batch * seq, BRANCH), F32),
        ],
        scratch_shapes=[
            pltpu.VMEM((kdim, kdim + sw4), BF16),
            pltpu.VMEM((sw4, kdim), BF16),
            pltpu.VMEM((n_ctx + seq, V7X_LANES), F32),
            pltpu.VMEM((nj, kdim), F32),
            pltpu.VMEM((nj, kdim), F32),
            pltpu.VMEM((4, nj * ntile, V7X_LANES), F32),
        ],
        compiler_params=_cparams(("arbitrary", "arbitrary")),
        name="ssm",
    )(projc, proj, c1, cc, decay, dsk_tiled)


def _merge_kernel(x_ref, gate_ref, gpost_ref,
                  naz_ref, pu_ref, pz_ref, cx_ref, cb_ref, cc_ref, cz_ref, sz_ref,
                  lg0_ref, lg1_ref, lg2_ref, lg3_ref,
                  pu_prev_ref, pu_next_ref, cx_prev_ref, cx_next_ref, cc_prev_ref, cc_next_ref,
                  ona_ref, yssm_ref,
                  poolw_ref, pscale_ref, convw_ref, gluw_ref, bgate_ref, wbr_ref, wo_ref,
                  o_ref, pad_ref, *, tm, rows_per_seq):
    i = pl.program_id(0)
    tiles_per_seq = rows_per_seq // tm
    ti = i % tiles_per_seq
    has_prev = ti > 0
    has_next = ti < tiles_per_seq - 1
    t0 = ti * tm
    pos = t0 + lax.broadcasted_iota(jnp.int32, (tm, 1), 0)

    pad_ref[0:HALO, :] = jnp.where(has_prev, pu_prev_ref[...].astype(F32), 0.0)
    pad_ref[HALO:HALO + tm, :] = pu_ref[...].astype(F32)
    pad_ref[HALO + tm:2 * HALO + tm, :] = jnp.where(has_next, pu_next_ref[...].astype(F32), 0.0)
    mixed = []
    for gi, w in enumerate(POOL_WINDOWS):
        cs = slice(gi * POOL_GROUP_DIM, (gi + 1) * POOL_GROUP_DIM)
        acc = None
        for dlt in range(-(w // 2), w - w // 2):
            v = pad_ref[HALO + dlt:HALO + dlt + tm, cs]
            acc = v if acc is None else acc + v
        lo = jnp.maximum(pos - w // 2, 0)
        hi = jnp.minimum(pos + w - w // 2, rows_per_seq)
        cnt = (hi - lo).astype(F32)
        pooled = acc / cnt - pad_ref[HALO:HALO + tm, cs]
        mixed.append(jnp.dot(pooled.astype(BF16), poolw_ref[gi], preferred_element_type=F32))
    o_pool = jnp.concatenate(mixed, axis=-1) * pscale_ref[...]

    pad_ref[0:HALO, :] = jnp.where(
        has_prev, cc_prev_ref[...].astype(F32) * cx_prev_ref[...].astype(F32), 0.0)
    pad_ref[HALO:HALO + tm, :] = cc_ref[...].astype(F32) * cx_ref[...].astype(F32)
    pad_ref[HALO + tm:2 * HALO + tm, :] = jnp.where(
        has_next, cc_next_ref[...].astype(F32) * cx_next_ref[...].astype(F32), 0.0)
    cw = convw_ref[...]
    conv = (pad_ref[HALO - 1:HALO - 1 + tm, :] * cw[0:1] + pad_ref[HALO:HALO + tm, :] * cw[1:2]
            + pad_ref[HALO + 1:HALO + 1 + tm, :] * cw[2:3])
    o_conv = cb_ref[...].astype(F32) * conv

    g = jax.nn.gelu(yssm_ref[...]).astype(BF16)
    gg = jnp.dot(g, gluw_ref[...], preferred_element_type=F32)
    o_ssm = gg[:, 0:BRANCH] * jax.nn.sigmoid(gg[:, BRANCH:2 * BRANCH])

    outs = (ona_ref[...].astype(F32), o_pool, o_conv, o_ssm)
    zs = (naz_ref, pz_ref, cz_ref, sz_ref)
    lgs = (lg0_ref, lg1_ref, lg2_ref, lg3_ref)
    merged = None
    for bi in range(N_BRANCHES):
        z = zs[bi][...].astype(F32)
        a = (outs[bi] * (z * jax.nn.sigmoid(z))).astype(BF16)
        br = jnp.dot(a, wbr_ref[bi * BRANCH:(bi + 1) * BRANCH, :], preferred_element_type=F32)
        gt = jax.nn.sigmoid(lgs[bi][...].astype(F32) + bgate_ref[:, bi * D_MODEL:(bi + 1) * D_MODEL])
        term = gt * br
        merged = term if merged is None else merged + term
    y = jnp.dot(merged.astype(BF16), wo_ref[...], preferred_element_type=F32)
    ms = jnp.mean(y * y, axis=-1, keepdims=True)
    yn = y * lax.rsqrt(ms + RMS_EPS) * gpost_ref[...]
    o_ref[...] = x_ref[...] + gate_ref[0] * yn


def _merge_call(x2, gate, g_post, proj, o_na, y_ssm, pool_w, pool_scale, conv_w, glu_w, b_gate,
                w_br, w_o, layer, rows_per_seq, tm):
    r, d = x2.shape
    tiles_per_seq = rows_per_seq // tm
    hb = tm // HALO
    n_halo_blocks = r // HALO

    def col(c):
        return pl.BlockSpec((tm, BRANCH), lambda i: (i, c))

    def lg(c):
        return pl.BlockSpec((tm, D_MODEL), lambda i: (i, COL_MERGE * BRANCH // D_MODEL + c))

    def prev(c):
        return pl.BlockSpec((HALO, BRANCH), lambda i: (jnp.maximum(i * hb - 1, 0), c))

    def nxt(c):
        return pl.BlockSpec((HALO, BRANCH), lambda i: (jnp.minimum((i + 1) * hb, n_halo_blocks - 1), c))

    in_specs = [
        pl.BlockSpec((tm, d), lambda i: (i, 0)),
        pl.BlockSpec((1, 1, d), lambda i: (i // tiles_per_seq, 0, 0)),
        pl.BlockSpec((1, d), lambda i: (0, 0)),
        col(COL_NA_Z), col(COL_POOL_U), col(COL_POOL_Z), col(COL_CONV_X), col(COL_CONV_B),
        col(COL_CONV_C), col(COL_CONV_Z), col(COL_SSM_Z),
        lg(0), lg(1), lg(2), lg(3),
        prev(COL_POOL_U), nxt(COL_POOL_U), prev(COL_CONV_X), nxt(COL_CONV_X),
        prev(COL_CONV_C), nxt(COL_CONV_C),
        pl.BlockSpec((tm, BRANCH), lambda i: (i, 0)),
        pl.BlockSpec((tm, BRANCH), lambda i: (i, 0)),
        _layer_spec(pool_w.shape, layer), _const_spec(pool_scale.shape), _const_spec(conv_w.shape),
        _layer_spec(glu_w.shape, layer), _const_spec(b_gate.shape), _layer_spec(w_br.shape, layer),
        _layer_spec(w_o.shape, layer),
    ]
    args = [x2, gate, g_post] + [proj] * 8 + [proj] * 4 + [proj] * 6 + [
        o_na, y_ssm, pool_w, pool_scale, conv_w, glu_w, b_gate, w_br, w_o]
    return pl.pallas_call(
        functools.partial(_merge_kernel, tm=tm, rows_per_seq=rows_per_seq),
        grid=(r // tm,),
        in_specs=in_specs,
        out_specs=pl.BlockSpec((tm, d), lambda i: (i, 0)),
        out_shape=jax.ShapeDtypeStruct((r, d), F32),
        scratch_shapes=[pltpu.VMEM((tm + 2 * HALO, BRANCH), F32)],
        compiler_params=_cparams(("arbitrary",)),
        name="merge",
    )(*args)


def _tile_rows(rows, pref):
    t = min(rows, pref)
    assert rows % t == 0
    return t


def kernel(x, c, ctx, c_ctx, w_mod, b_mod, g_pre, g_post, w_in, b_gate, na_rpb, pool_w, pool_scale,
           conv_w, ssm_a_re, ssm_a_im, ssm_log_dt, ssm_b_re, ssm_b_im, ssm_c_re, ssm_c_im, ssm_d,
           glu_w, w_br, w_o):
    batch, seq, d = x.shape
    n_ctx = ctx.shape[1]
    depth = w_mod.shape[0]
    assert d == D_MODEL and seq % (GRID_W * NA_WIN_ROWS) == 0 and batch + 1 <= 8
    assert n_ctx % HALO == 0 and w_in.shape[-1] == IN_TOTAL

    c8 = jnp.concatenate([c, c_ctx[None], jnp.zeros((7 - batch, d), F32)], axis=0)
    mod = _mod_call(c8, w_mod, b_mod)

    x2 = x.reshape(batch * seq, d)
    xc2 = ctx.reshape(batch * n_ctx, d)
    tm_in = _tile_rows(seq, 1024)
    tm_in_c = _tile_rows(batch * n_ctx, 1024)
    tm_mg = _tile_rows(seq, 256)
    tm_mg_c = _tile_rows(n_ctx, 256)

    bias_tbls = jax.vmap(_na_bias_table)(na_rpb)
    c1s, ccs, decays = jax.vmap(_ssm_weights)(ssm_a_re, ssm_a_im, ssm_log_dt, ssm_b_re, ssm_b_im,
                                              ssm_c_re, ssm_c_im)
    dsks = jnp.tile(ssm_d.astype(F32).reshape(depth, SSM_TILES, 1, V7X_LANES), (1, 1, 1, SSM_CHUNK))
    w_in_bf, pool_w_bf, glu_w_bf = w_in.astype(BF16), pool_w.astype(BF16), glu_w.astype(BF16)
    w_br_bf, w_o_bf = w_br.astype(BF16), w_o.astype(BF16)

    for i in range(depth):
        with_ctx_out = i < depth - 1
        shift, scale, gate = (mod[i, :, k * d:(k + 1) * d] for k in range(3))
        lat = lambda v: v[:batch].reshape(batch, 1, d)
        cx = lambda v: v[batch:batch + 1].reshape(1, 1, d)
        gp = g_pre[i].reshape(1, d)
        gq = g_post[i].reshape(1, d)

        proj = _inproj_call(x2, lat(scale), lat(shift), gp, w_in_bf, i, seq, tm_in, 2048)
        projc = _inproj_call(xc2, cx(scale), cx(shift), gp, w_in_bf, i, batch * n_ctx, tm_in_c, 2048)

        o_na = _na_call(proj, projc, bias_tbls, i, batch, seq, n_ctx)
        yc_ssm, y_ssm = _ssm_call(proj, projc, c1s, ccs, decays, dsks, i, batch, seq, n_ctx)

        mw = (pool_w_bf, pool_scale[i].reshape(1, BRANCH), conv_w[i], glu_w_bf,
              b_gate[i].reshape(1, N_BRANCHES * d), w_br_bf, w_o_bf, i)
        x2 = _merge_call(x2, lat(gate), gq, proj, o_na, y_ssm, *mw, seq, tm_mg)
        if with_ctx_out:
            oc_na = _ctx_attn_call(projc, batch, n_ctx)
            gate_c = jnp.broadcast_to(cx(gate), (batch, 1, d))
            xc2 = _merge_call(xc2, gate_c, gq, projc, oc_na, yc_ssm, *mw, n_ctx, tm_mg_c)
    return x2.reshape(batch, seq, d)
```

```python
import functools

import jax
import jax.numpy as jnp
import numpy as np
from jax import lax
from jax.experimental import pallas as pl
from jax.experimental.pallas import tpu as pltpu

F32 = jnp.float32
BF16 = jnp.bfloat16

D_MODEL = 2048
GRID_W = 64
RMS_EPS = 1e-6
NEG_INF = -1e30
BRANCH = 512
N_BRANCHES = 4
NA_HEADS = 8
NA_HEAD_DIM = 64
NA_WIN_ROWS = 8
NA_WIN_COLS = 16
NA_GROUP_ROWS = 4
NA_KEY_ROWS = 12
POOL_WINDOWS = (2, 4, 8, 16)
POOL_GROUP_DIM = 128
SSM_GROUPS = 32
SSM_GROUP_DIM = 16
SSM_STATE = 64
IN_TOTAL = 12 * BRANCH + N_BRANCHES * D_MODEL
COL_Q, COL_K, COL_V, COL_NA_Z = 0, 1, 2, 3
COL_POOL_U, COL_POOL_Z = 4, 5
COL_CONV_X, COL_CONV_B, COL_CONV_C, COL_CONV_Z = 6, 7, 8, 9
COL_SSM_U, COL_SSM_Z = 10, 11
COL_MERGE = 12

V7X_LANES = 128
SSM_CHUNK = 8
SSM_LANE_GROUPS = V7X_LANES // SSM_GROUP_DIM
SSM_TILES = BRANCH // V7X_LANES
HALO = 16
VMEM_LIMIT = 56 * 1024 * 1024


def _cparams(sem):
    return pltpu.CompilerParams(dimension_semantics=sem, vmem_limit_bytes=VMEM_LIMIT)


def _const_spec(shape):
    nd = len(shape)
    return pl.BlockSpec(shape, lambda *_: (0,) * nd, pipeline_mode=pl.Buffered(1))


def _layer_spec(stacked_shape, layer):
    nd = len(stacked_shape) - 1
    return pl.BlockSpec((None,) + tuple(stacked_shape[1:]), lambda *_: (layer,) + (0,) * nd,
                        pipeline_mode=pl.Buffered(1))


def _mod_kernel(c_ref, w_ref, b_ref, o_ref):
    c = c_ref[...]
    a = (c * jax.nn.sigmoid(c)).astype(BF16)
    o_ref[0] = jnp.dot(a, w_ref[0].astype(BF16), preferred_element_type=F32) + b_ref[0]


def _mod_call(c8, w_mod, b_mod):
    depth, d, n3 = w_mod.shape
    tn = 512
    return pl.pallas_call(
        _mod_kernel,
        grid=(depth, n3 // tn),
        in_specs=[
            pl.BlockSpec((8, d), lambda l, j: (0, 0)),
            pl.BlockSpec((1, d, tn), lambda l, j: (l, 0, j)),
            pl.BlockSpec((1, 1, tn), lambda l, j: (l, 0, j)),
        ],
        out_specs=pl.BlockSpec((1, 8, tn), lambda l, j: (l, 0, j)),
        out_shape=jax.ShapeDtypeStruct((depth, 8, n3), F32),
        compiler_params=_cparams(("arbitrary", "arbitrary")),
        name="mod",
    )(c8, w_mod, b_mod.reshape(depth, 1, n3))


def _inproj_kernel(x_ref, scale_ref, shift_ref, g_ref, w_ref, o_ref, h_ref):
    @pl.when(pl.program_id(1) == 0)
    def _():
        x = x_ref[...]
        ms = jnp.mean(x * x, axis=-1, keepdims=True)
        gain = g_ref[...] * (1.0 + scale_ref[0])
        h_ref[...] = (x * lax.rsqrt(ms + RMS_EPS) * gain + shift_ref[0]).astype(BF16)

    o_ref[...] = jnp.dot(h_ref[...], w_ref[...], preferred_element_type=F32).astype(o_ref.dtype)


def _inproj_call(x2, scale, shift, g_pre, w_bf, layer, rows_per_seq, tm, tn):
    r, d = x2.shape
    n = w_bf.shape[2]
    tiles_per_seq = rows_per_seq // tm
    return pl.pallas_call(
        _inproj_kernel,
        grid=(r // tm, n // tn),
        in_specs=[
            pl.BlockSpec((tm, d), lambda i, j: (i, 0)),
            pl.BlockSpec((1, 1, d), lambda i, j: (i // tiles_per_seq, 0, 0)),
            pl.BlockSpec((1, 1, d), lambda i, j: (i // tiles_per_seq, 0, 0)),
            pl.BlockSpec((1, d), lambda i, j: (0, 0)),
            pl.BlockSpec((None, d, tn), lambda i, j: (layer, 0, j)),
        ],
        out_specs=pl.BlockSpec((tm, tn), lambda i, j: (i, j)),
        out_shape=jax.ShapeDtypeStruct((r, n), BF16),
        scratch_shapes=[pltpu.VMEM((tm, d), BF16)],
        compiler_params=_cparams(("arbitrary", "arbitrary")),
        name="inproj",
    )(x2, scale, shift, g_pre, w_bf)


def _head_mask(rows):
    lane = lax.broadcasted_iota(jnp.int32, (rows, V7X_LANES), 1)
    return lane < NA_HEAD_DIM


def _na_kernel(q_ref, k_ref, v_ref, kc_ref, vc_ref, t_ref, o_ref, bias_ref, *, grid_rows):
    @pl.when((pl.program_id(0) == 0) & (pl.program_id(1) == 0))
    def _():
        _na_expand_bias(t_ref, bias_ref)

    r0 = pl.program_id(1) * NA_GROUP_ROWS
    nq = NA_GROUP_ROWS * GRID_W
    first_half = _head_mask(nq)
    scale = NA_HEAD_DIM ** -0.5
    nt = (((1,), (1,)), ((), ()))
    gs = jnp.clip(r0 - NA_WIN_ROWS // 2, 0, grid_rows - NA_KEY_ROWS)
    cls = jnp.where(r0 == 0, 0, jnp.where(r0 == grid_rows - NA_GROUP_ROWS, 2, 1))
    k0 = pl.multiple_of(gs * GRID_W, GRID_W)
    for hp in range(NA_HEADS // 2):
        cs = slice(hp * V7X_LANES, (hp + 1) * V7X_LANES)
        qp = q_ref[:, cs] * jnp.asarray(scale, BF16)
        kp = k_ref[pl.ds(k0, NA_KEY_ROWS * GRID_W), cs]
        vp = v_ref[pl.ds(k0, NA_KEY_ROWS * GRID_W), cs]
        kcp = kc_ref[:, cs]
        vcp = vc_ref[:, cs]
        zero = jnp.zeros_like(qp)
        q2 = jnp.concatenate([jnp.where(first_half, qp, zero), jnp.where(first_half, zero, qp)], axis=0)
        s_b = lax.dot_general(q2, kp, nt, preferred_element_type=F32) + bias_ref[cls, hp]
        s_c = lax.dot_general(q2, kcp, nt, preferred_element_type=F32)
        m = jnp.maximum(jnp.max(s_b, axis=-1, keepdims=True), jnp.max(s_c, axis=-1, keepdims=True))
        p_b = jnp.exp(s_b - m)
        p_c = jnp.exp(s_c - m)
        denom = jnp.sum(p_b, axis=-1, keepdims=True) + jnp.sum(p_c, axis=-1, keepdims=True)
        o = (jnp.dot(p_b.astype(BF16), vp, preferred_element_type=F32)
             + jnp.dot(p_c.astype(BF16), vcp, preferred_element_type=F32)) / denom
        o_ref[:, cs] = jnp.where(first_half, o[0:nq], o[nq:]).astype(o_ref.dtype)


def _na_call(proj, projc, bias_tbls, layer, batch, seq, n_ctx):
    grid_rows = seq // GRID_W
    assert grid_rows % NA_GROUP_ROWS == 0 and grid_rows >= NA_KEY_ROWS
    tq = NA_GROUP_ROWS * GRID_W
    qb = seq // tq
    return pl.pallas_call(
        functools.partial(_na_kernel, grid_rows=grid_rows),
        grid=(batch, qb),
        in_specs=[
            pl.BlockSpec((tq, BRANCH), lambda b, r: (b * qb + r, COL_Q)),
            pl.BlockSpec((seq, BRANCH), lambda b, r: (b, COL_K)),
            pl.BlockSpec((seq, BRANCH), lambda b, r: (b, COL_V)),
            pl.BlockSpec((n_ctx, BRANCH), lambda b, r: (b, COL_K)),
            pl.BlockSpec((n_ctx, BRANCH), lambda b, r: (b, COL_V)),
            _layer_spec(bias_tbls.shape, layer),
        ],
        out_specs=pl.BlockSpec((tq, BRANCH), lambda b, r: (b * qb + r, 0)),
        out_shape=jax.ShapeDtypeStruct((batch * seq, BRANCH), BF16),
        scratch_shapes=[pltpu.VMEM((3, NA_HEADS // 2, 2 * tq, NA_KEY_ROWS * GRID_W), F32)],
        compiler_params=_cparams(("arbitrary", "arbitrary")),
        name="na_attn",
    )(proj, proj, proj, projc, projc, bias_tbls)


def _ctx_attn_kernel(q_ref, k_ref, v_ref, o_ref):
    n = q_ref.shape[0]
    first_half = _head_mask(n)
    scale = NA_HEAD_DIM ** -0.5
    nt = (((1,), (1,)), ((), ()))
    for hp in range(NA_HEADS // 2):
        cs = slice(hp * V7X_LANES, (hp + 1) * V7X_LANES)
        qp = q_ref[:, cs] * jnp.asarray(scale, BF16)
        kp = k_ref[:, cs]
        vp = v_ref[:, cs]
        zero = jnp.zeros_like(qp)
        q2 = jnp.concatenate([jnp.where(first_half, qp, zero), jnp.where(first_half, zero, qp)], axis=0)
        s = lax.dot_general(q2, kp, nt, preferred_element_type=F32)
        m = jnp.max(s, axis=-1, keepdims=True)
        p = jnp.exp(s - m)
        denom = jnp.sum(p, axis=-1, keepdims=True)
        o = jnp.dot(p.astype(BF16), vp, preferred_element_type=F32) / denom
        o_ref[:, cs] = jnp.where(first_half, o[0:n], o[n:]).astype(o_ref.dtype)


def _ctx_attn_call(projc, batch, n_ctx):
    return pl.pallas_call(
        _ctx_attn_kernel,
        grid=(batch,),
        in_specs=[
            pl.BlockSpec((n_ctx, BRANCH), lambda b: (b, COL_Q)),
            pl.BlockSpec((n_ctx, BRANCH), lambda b: (b, COL_K)),
            pl.BlockSpec((n_ctx, BRANCH), lambda b: (b, COL_V)),
        ],
        out_specs=pl.BlockSpec((n_ctx, BRANCH), lambda b: (b, 0)),
        out_shape=jax.ShapeDtypeStruct((batch * n_ctx, BRANCH), BF16),
        compiler_params=_cparams(("arbitrary",)),
        name="ctx_attn",
    )(projc, projc, projc)


def _na_bias_table(rpb):
    col = np.arange(GRID_W)
    col_start = np.clip(col - NA_WIN_COLS // 2, 0, GRID_W - NA_WIN_COLS)
    in_win = (col[None, :] >= col_start[:, None]) & (col[None, :] < col_start[:, None] + NA_WIN_COLS)
    dcol = np.clip(col[None, :] - col[:, None] + (NA_WIN_COLS - 1), 0, 2 * NA_WIN_COLS - 2)
    onehot = (dcol[..., None] == np.arange(2 * NA_WIN_COLS - 1)).astype(np.float32)
    t = jnp.einsum('hdc,qwc->hqdw', rpb.astype(F32), onehot, precision=lax.Precision.HIGHEST)
    t = jnp.where(in_win[None, :, None, :], t, NEG_INF)
    t = t.transpose(0, 2, 1, 3)
    return jnp.concatenate([t, t], axis=-1)


def _na_band_rows(cls, i):
    lo = (0, i, NA_KEY_ROWS - NA_WIN_ROWS)[cls]
    off = (NA_WIN_ROWS - 1 - i, NA_WIN_ROWS // 2 - 1 - i, NA_WIN_ROWS // 2 - 1 - lo - i)[cls]
    return lo, off


def _na_expand_bias(t_ref, bias_ref):
    lane = lax.broadcasted_iota(jnp.int32, (GRID_W, V7X_LANES), 1)
    left = lane < GRID_W
    neg = jnp.full((GRID_W, V7X_LANES), NEG_INF, F32)
    for cls in range(3):
        for hp in range(NA_HEADS // 2):
            for e in range(2):
                for i in range(NA_GROUP_ROWS):
                    lo, off = _na_band_rows(cls, i)
                    r0 = (e * NA_GROUP_ROWS + i) * GRID_W
                    for kt in range(NA_KEY_ROWS // 2):
                        halves = []
                        for kw in (2 * kt, 2 * kt + 1):
                            inside = lo <= kw < lo + NA_WIN_ROWS
                            halves.append(t_ref[2 * hp + e, kw + off] if inside else neg)
                        bias_ref[cls, hp, r0:r0 + GRID_W, kt * V7X_LANES:(kt + 1) * V7X_LANES] = (
                            jnp.where(left, halves[0], halves[1]))


def _cmul(ar, ai, br, bi):
    return ar * br - ai * bi, ar * bi + ai * br


def _ssm_weights(a_re, a_im, log_dt, b_re, b_im, c_re, c_im):
    t_len = SSM_CHUNK
    p, hg = SSM_STATE, SSM_GROUP_DIM
    lg = SSM_LANE_GROUPS
    kf, ef, mf, dec = [], [], [], []
    for d in range(2):
        ar, ai = a_re[d].astype(F32), a_im[d].astype(F32)
        dt = jnp.exp(log_dt[d].astype(F32))[:, None]
        mag = jnp.exp(ar * dt)
        abr, abi = mag * jnp.cos(ai * dt), mag * jnp.sin(ai * dt)
        den = ar * ar + ai * ai
        nr, ni = abr - 1.0, abi
        fr = (nr * ar + ni * ai) / den
        fi = (ni * ar - nr * ai) / den
        bbr, bbi = _cmul(fr[..., None], fi[..., None], b_re[d].astype(F32), b_im[d].astype(F32))
        pws = [(jnp.ones_like(abr), jnp.zeros_like(abr))]
        for _ in range(t_len):
            pws.append(_cmul(pws[-1][0], pws[-1][1], abr, abi))
        pr = jnp.stack([x[0] for x in pws])
        pi = jnp.stack([x[1] for x in pws])
        cr, ci = c_re[d].astype(F32), c_im[d].astype(F32)
        mr, mi = _cmul(cr[None], ci[None], pr[:t_len, :, None, :], pi[:t_len, :, None, :])
        kf.append(jnp.einsum('tghp,gpk->tghk', mr, bbr) - jnp.einsum('tghp,gpk->tghk', mi, bbi))
        e_pw = [pws[t_len - 1 - t] if d == 0 else pws[t] for t in range(t_len)]
        per, pei = jnp.stack([x[0] for x in e_pw]), jnp.stack([x[1] for x in e_pw])
        er, ei = _cmul(per[..., None], pei[..., None], bbr[None], bbi[None])
        ef.append((er, ei))
        c_pw = [pws[t + 1] if d == 0 else pws[t_len - t] for t in range(t_len)]
        pcr, pci = jnp.stack([x[0] for x in c_pw]), jnp.stack([x[1] for x in c_pw])
        m2r, m2i = _cmul(cr[None], ci[None], pcr[:, :, None, :], pci[:, :, None, :])
        mf.append((m2r, m2i))
        dec.append((pr[t_len], pi[t_len]))
    kc = [k.reshape(t_len, SSM_TILES, lg, hg, hg).transpose(0, 1, 4, 2, 3).reshape(
        t_len, SSM_TILES, hg, V7X_LANES) for k in kf]
    z = ([kc[1][t_len - 1 - i] for i in range(t_len - 1)] + [kc[0][0] + kc[1][0]]
         + [kc[0][i] for i in range(1, t_len)])
    ktoe = jnp.stack([jnp.stack(z[t_len - 1 - tp:2 * t_len - 1 - tp]) for tp in range(t_len)])
    parts = [ktoe.transpose(2, 0, 3, 1, 4).reshape(SSM_TILES, t_len, hg, t_len * V7X_LANES)]
    for d in range(2):
        for comp in ef[d]:
            parts.append(comp.reshape(t_len, SSM_TILES, lg, p, hg).transpose(1, 0, 4, 2, 3).reshape(
                SSM_TILES, t_len, hg, lg * p))
    c1 = jnp.concatenate(parts, axis=-1)
    rows = []
    for d in range(2):
        m2r, m2i = mf[d]
        for comp in (m2r, -m2i):
            rows.append(comp.reshape(t_len, SSM_TILES, lg, hg, p).transpose(1, 4, 0, 2, 3).reshape(
                SSM_TILES, p, t_len * V7X_LANES))
    cc = jnp.stack(rows, axis=1)
    decay = jnp.stack([x.reshape(SSM_TILES, SSM_TILES, V7X_LANES) for d in range(2) for x in dec[d]], axis=1)
    return c1, cc, decay


def _ssm_kernel(uc_ref, u_ref, c1_ref, cc_ref, dec_ref, dsk_ref, yc_ref, y_ref,
                w1_ref, wc_ref, uf_ref, lhs_ref, yi_ref, es_ref, *, n_ctx, seq):
    t_len = SSM_CHUNK
    nc, lc = n_ctx // t_len, seq // t_len
    nj = nc + lc
    kdim = t_len * V7X_LANES
    sw = SSM_LANE_GROUPS * SSM_STATE
    ntile = sw // V7X_LANES
    hg = SSM_GROUP_DIM

    @pl.when(pl.program_id(1) == 0)
    def _():
        lane1 = lax.broadcasted_iota(jnp.int32, (1, kdim + 4 * sw), 1)
        grp1 = jnp.where(lane1 < kdim, (lane1 % V7X_LANES) // hg, ((lane1 - kdim) % sw) // SSM_STATE)
        lanec = lax.broadcasted_iota(jnp.int32, (1, kdim), 1)
        grpc = (lanec % V7X_LANES) // hg
        for t in range(t_len):
            row = c1_ref[0, t]
            for g in range(SSM_LANE_GROUPS):
                r0 = (t * SSM_LANE_GROUPS + g) * hg
                w1_ref[r0:r0 + hg, :] = jnp.where(grp1 == g, row, 0.0).astype(BF16)
        for part in range(4):
            blk = cc_ref[0, part]
            for g in range(SSM_LANE_GROUPS):
                r0 = part * sw + g * SSM_STATE
                wc_ref[r0:r0 + SSM_STATE, :] = jnp.where(grpc == g, blk, 0.0).astype(BF16)

    uf_ref[0:n_ctx, :] = uc_ref[...].astype(F32)
    uf_ref[n_ctx:n_ctx + seq, :] = u_ref[...].astype(F32)
    for t in range(t_len):
        lhs_ref[:, t * V7X_LANES:(t + 1) * V7X_LANES] = uf_ref[pl.ds(t, nj, stride=t_len), :]
    lhs = lhs_ref[...].astype(BF16)
    yi_ref[...] = jnp.dot(lhs, w1_ref[:, 0:kdim], preferred_element_type=F32) + lhs_ref[...] * dsk_ref[0]
    for c in range(4):
        e = jnp.dot(lhs, w1_ref[:, kdim + c * sw:kdim + (c + 1) * sw], preferred_element_type=F32)
        for k in range(ntile):
            es_ref[c, pl.ds(k, nj, stride=ntile), :] = e[:, k * V7X_LANES:(k + 1) * V7X_LANES]

    afr, afi, abr, abi = dec_ref[0, 0], dec_ref[0, 1], dec_ref[0, 2], dec_ref[0, 3]

    def step(k, st):
        sfr, sfi, sbr, sbi = st
        jb = jnp.where(k < nc, nc - 1 - k, nj - 1 - (k - nc))
        rf = pl.ds(pl.multiple_of(k * ntile, ntile), ntile)
        rb = pl.ds(pl.multiple_of(jb * ntile, ntile), ntile)
        efr, efi = es_ref[0, rf, :], es_ref[1, rf, :]
        ebr, ebi = es_ref[2, rb, :], es_ref[3, rb, :]
        es_ref[0, rf, :] = sfr
        es_ref[1, rf, :] = sfi
        es_ref[2, rb, :] = sbr
        es_ref[3, rb, :] = sbi
        nfr, nfi = _cmul(afr, afi, sfr, sfi)
        nbr, nbi = _cmul(abr, abi, sbr, sbi)
        return nfr + efr, nfi + efi, nbr + ebr, nbi + ebi

    z = jnp.zeros((ntile, V7X_LANES), F32)
    lax.fori_loop(0, nj, step, (z, z, z, z))

    carried = jnp.concatenate(
        [es_ref[c, pl.ds(k, nj, stride=ntile), :] for c in range(4) for k in range(ntile)], axis=1)
    yi_ref[...] += jnp.dot(carried.astype(BF16), wc_ref[...], preferred_element_type=F32)
    for t in range(t_len):
        uf_ref[pl.ds(t, nj, stride=t_len), :] = yi_ref[:, t * V7X_LANES:(t + 1) * V7X_LANES]
    yc_ref[...] = uf_ref[0:n_ctx, :]
    y_ref[...] = uf_ref[n_ctx:n_ctx + seq, :]


def _ssm_call(proj, projc, c1, cc, decay, dsk_tiled, layer, batch, seq, n_ctx):
    t_len = SSM_CHUNK
    nj = (n_ctx + seq) // t_len
    kdim = t_len * V7X_LANES
    sw4 = 4 * SSM_LANE_GROUPS * SSM_STATE
    ntile = sw4 // 4 // V7X_LANES
    col0 = COL_SSM_U * BRANCH // V7X_LANES
    return pl.pallas_call(
        functools.partial(_ssm_kernel, n_ctx=n_ctx, seq=seq),
        grid=(SSM_TILES, batch),
        in_specs=[
            pl.BlockSpec((n_ctx, V7X_LANES), lambda s, b: (b, col0 + s)),
            pl.BlockSpec((seq, V7X_LANES), lambda s, b: (b, col0 + s)),
            pl.BlockSpec((None, 1, t_len, SSM_GROUP_DIM, kdim + sw4), lambda s, b: (layer, s, 0, 0, 0)),
            pl.BlockSpec((None, 1, 4, SSM_STATE, kdim), lambda s, b: (layer, s, 0, 0, 0)),
            pl.BlockSpec((None, 1, 4, ntile, V7X_LANES), lambda s, b: (layer, s, 0, 0, 0)),
            pl.BlockSpec((None, 1, 1, kdim), lambda s, b: (layer, s, 0, 0)),
        ],
        out_specs=[
            pl.BlockSpec((n_ctx, V7X_LANES), lambda s, b: (b, s)),
            pl.BlockSpec((seq, V7X_LANES), lambda s, b: (b, s)),
        ],
        out_shape=[
            jax.ShapeDtypeStruct((batch * n_ctx, BRANCH), F32),
            jax.ShapeDtypeStruct((batch * seq, BRANCH), F32),
        ],
        scratch_shapes=[
            pltpu.VMEM((kdim, kdim + sw4), BF16),
            pltpu.VMEM((sw4, kdim), BF16),
            pltpu.VMEM((n_ctx + seq, V7X_LANES), F32),
            pltpu.VMEM((nj, kdim), F32),
            pltpu.VMEM((nj, kdim), F32),
            pltpu.VMEM((4, nj * ntile, V7X_LANES), F32),
        ],
        compiler_params=_cparams(("arbitrary", "arbitrary")),
        name="ssm",
    )(projc, proj, c1, cc, decay, dsk_tiled)


def _merge_kernel(x_ref, gate_ref, gpost_ref,
                  naz_ref, pu_ref, pz_ref, cx_ref, cb_ref, cc_ref, cz_ref, sz_ref,
                  lg0_ref, lg1_ref, lg2_ref, lg3_ref,
                  pu_prev_ref, pu_next_ref, cx_prev_ref, cx_next_ref, cc_prev_ref, cc_next_ref,
                  ona_ref, yssm_ref,
                  poolw_ref, pscale_ref, convw_ref, gluw_ref, bgate_ref, wbr_ref, wo_ref,
                  o_ref, pad_ref, *, tm, rows_per_seq):
    i = pl.program_id(0)
    tiles_per_seq = rows_per_seq // tm
    ti = i % tiles_per_seq
    has_prev = ti > 0
    has_next = ti < tiles_per_seq - 1
    t0 = ti * tm
    pos = t0 + lax.broadcasted_iota(jnp.int32, (tm, 1), 0)

    pad_ref[0:HALO, :] = jnp.where(has_prev, pu_prev_ref[...].astype(F32), 0.0)
    pad_ref[HALO:HALO + tm, :] = pu_ref[...].astype(F32)
    pad_ref[HALO + tm:2 * HALO + tm, :] = jnp.where(has_next, pu_next_ref[...].astype(F32), 0.0)
    mixed = []
    for gi, w in enumerate(POOL_WINDOWS):
        cs = slice(gi * POOL_GROUP_DIM, (gi + 1) * POOL_GROUP_DIM)
        acc = None
        for dlt in range(-(w // 2), w - w // 2):
            v = pad_ref[HALO + dlt:HALO + dlt + tm, cs]
            acc = v if acc is None else acc + v
        lo = jnp.maximum(pos - w // 2, 0)
        hi = jnp.minimum(pos + w - w // 2, rows_per_seq)
        cnt = (hi - lo).astype(F32)
        pooled = acc / cnt - pad_ref[HALO:HALO + tm, cs]
        mixed.append(jnp.dot(pooled.astype(BF16), poolw_ref[gi], preferred_element_type=F32))
    o_pool = jnp.concatenate(mixed, axis=-1) * pscale_ref[...]

    pad_ref[0:HALO, :] = jnp.where(
        has_prev, cc_prev_ref[...].astype(F32) * cx_prev_ref[...].astype(F32), 0.0)
    pad_ref[HALO:HALO + tm, :] = cc_ref[...].astype(F32) * cx_ref[...].astype(F32)
    pad_ref[HALO + tm:2 * HALO + tm, :] = jnp.where(
        has_next, cc_next_ref[...].astype(F32) * cx_next_ref[...].astype(F32), 0.0)
    cw = convw_ref[...]
    conv = (pad_ref[HALO - 1:HALO - 1 + tm, :] * cw[0:1] + pad_ref[HALO:HALO + tm, :] * cw[1:2]
            + pad_ref[HALO + 1:HALO + 1 + tm, :] * cw[2:3])
    o_conv = cb_ref[...].astype(F32) * conv

    g = jax.nn.gelu(yssm_ref[...]).astype(BF16)
    gg = jnp.dot(g, gluw_ref[...], preferred_element_type=F32)
    o_ssm = gg[:, 0:BRANCH] * jax.nn.sigmoid(gg[:, BRANCH:2 * BRANCH])

    outs = (ona_ref[...].astype(F32), o_pool, o_conv, o_ssm)
    zs = (naz_ref, pz_ref, cz_ref, sz_ref)
    lgs = (lg0_ref, lg1_ref, lg2_ref, lg3_ref)
    merged = None
    for bi in range(N_BRANCHES):
        z = zs[bi][...].astype(F32)
        a = (outs[bi] * (z * jax.nn.sigmoid(z))).astype(BF16)
        br = jnp.dot(a, wbr_ref[bi * BRANCH:(bi + 1) * BRANCH, :], preferred_element_type=F32)
        gt = jax.nn.sigmoid(lgs[bi][...].astype(F32) + bgate_ref[:, bi * D_MODEL:(bi + 1) * D_MODEL])
        term = gt * br
        merged = term if merged is None else merged + term
    y = jnp.dot(merged.astype(BF16), wo_ref[...], preferred_element_type=F32)
    ms = jnp.mean(y * y, axis=-1, keepdims=True)
    yn = y * lax.rsqrt(ms + RMS_EPS) * gpost_ref[...]
    o_ref[...] = x_ref[...] + gate_ref[0] * yn


def _merge_call(x2, gate, g_post, proj, o_na, y_ssm, pool_w, pool_scale, conv_w, glu_w, b_gate,
                w_br, w_o, layer, rows_per_seq, tm):
    r, d = x2.shape
    tiles_per_seq = rows_per_seq // tm
    hb = tm // HALO
    n_halo_blocks = r // HALO

    def col(c):
        return pl.BlockSpec((tm, BRANCH), lambda i: (i, c))

    def lg(c):
        return pl.BlockSpec((tm, D_MODEL), lambda i: (i, COL_MERGE * BRANCH // D_MODEL + c))

    def prev(c):
        return pl.BlockSpec((HALO, BRANCH), lambda i: (jnp.maximum(i * hb - 1, 0), c))

    def nxt(c):
        return pl.BlockSpec((HALO, BRANCH), lambda i: (jnp.minimum((i + 1) * hb, n_halo_blocks - 1), c))

    in_specs = [
        pl.BlockSpec((tm, d), lambda i: (i, 0)),
        pl.BlockSpec((1, 1, d), lambda i: (i // tiles_per_seq, 0, 0)),
        pl.BlockSpec((1, d), lambda i: (0, 0)),
        col(COL_NA_Z), col(COL_POOL_U), col(COL_POOL_Z), col(COL_CONV_X), col(COL_CONV_B),
        col(COL_CONV_C), col(COL_CONV_Z), col(COL_SSM_Z),
        lg(0), lg(1), lg(2), lg(3),
        prev(COL_POOL_U), nxt(COL_POOL_U), prev(COL_CONV_X), nxt(COL_CONV_X),
        prev(COL_CONV_C), nxt(COL_CONV_C),
        pl.BlockSpec((tm, BRANCH), lambda i: (i, 0)),
        pl.BlockSpec((tm, BRANCH), lambda i: (i, 0)),
        _layer_spec(pool_w.shape, layer), _const_spec(pool_scale.shape), _const_spec(conv_w.shape),
        _layer_spec(glu_w.shape, layer), _const_spec(b_gate.shape), _layer_spec(w_br.shape, layer),
        _layer_spec(w_o.shape, layer),
    ]
    args = [x2, gate, g_post] + [proj] * 8 + [proj] * 4 + [proj] * 6 + [
        o_na, y_ssm, pool_w, pool_scale, conv_w, glu_w, b_gate, w_br, w_o]
    return pl.pallas_call(
        functools.partial(_merge_kernel, tm=tm, rows_per_seq=rows_per_seq),
        grid=(r // tm,),
        in_specs=in_specs,
        out_specs=pl.BlockSpec((tm, d), lambda i: (i, 0)),
        out_shape=jax.ShapeDtypeStruct((r, d), F32),
        scratch_shapes=[pltpu.VMEM((tm + 2 * HALO, BRANCH), F32)],
        compiler_params=_cparams(("arbitrary",)),
        name="merge",
    )(*args)


def _tile_rows(rows, pref):
    t = min(rows, pref)
    assert rows % t == 0
    return t


def kernel(x, c, ctx, c_ctx, w_mod, b_mod, g_pre, g_post, w_in, b_gate, na_rpb, pool_w, pool_scale,
           conv_w, ssm_a_re, ssm_a_im, ssm_log_dt, ssm_b_re, ssm_b_im, ssm_c_re, ssm_c_im, ssm_d,
           glu_w, w_br, w_o):
    batch, seq, d = x.shape
    n_ctx = ctx.shape[1]
    depth = w_mod.shape[0]
    assert d == D_MODEL and seq % (GRID_W * NA_WIN_ROWS) == 0 and batch + 1 <= 8
    assert n_ctx % HALO == 0 and w_in.shape[-1] == IN_TOTAL

    c8 = jnp.concatenate([c, c_ctx[None], jnp.zeros((7 - batch, d), F32)], axis=0)
    mod = _mod_call(c8, w_mod, b_mod)

    x2 = x.reshape(batch * seq, d)
    xc2 = ctx.reshape(batch * n_ctx, d)
    tm_in = _tile_rows(seq, 1024)
    tm_in_c = _tile_rows(batch * n_ctx, 1024)
    tm_mg = _tile_rows(seq, 256)
    tm_mg_c = _tile_rows(n_ctx, 256)

    bias_tbls = jax.vmap(_na_bias_table)(na_rpb)
    c1s, ccs, decays = jax.vmap(_ssm_weights)(ssm_a_re, ssm_a_im, ssm_log_dt, ssm_b_re, ssm_b_im,
                                              ssm_c_re, ssm_c_im)
    dsks = jnp.tile(ssm_d.astype(F32).reshape(depth, SSM_TILES, 1, V7X_LANES), (1, 1, 1, SSM_CHUNK))
    w_in_bf, pool_w_bf, glu_w_bf = w_in.astype(BF16), pool_w.astype(BF16), glu_w.astype(BF16)
    w_br_bf, w_o_bf = w_br.astype(BF16), w_o.astype(BF16)

    for i in range(depth):
        with_ctx_out = i < depth - 1
        shift, scale, gate = (mod[i, :, k * d:(k + 1) * d] for k in range(3))
        lat = lambda v: v[:batch].reshape(batch, 1, d)
        cx = lambda v: v[batch:batch + 1].reshape(1, 1, d)
        gp = g_pre[i].reshape(1, d)
        gq = g_post[i].reshape(1, d)

        proj = _inproj_call(x2, lat(scale), lat(shift), gp, w_in_bf, i, seq, tm_in, 2048)
        projc = _inproj_call(xc2, cx(scale), cx(shift), gp, w_in_bf, i, batch * n_ctx, tm_in_c, 2048)

        o_na = _na_call(proj, projc, bias_tbls, i, batch, seq, n_ctx)
        yc_ssm, y_ssm = _ssm_call(proj, projc, c1s, ccs, decays, dsks, i, batch, seq, n_ctx)

        mw = (pool_w_bf, pool_scale[i].reshape(1, BRANCH), conv_w[i], glu_w_bf,
              b_gate[i].reshape(1, N_BRANCHES * d), w_br_bf, w_o_bf, i)
        x2 = _merge_call(x2, lat(gate), gq, proj, o_na, y_ssm, *mw, seq, tm_mg)
        if with_ctx_out:
            oc_na = _ctx_attn_call(projc, batch, n_ctx)
            gate_c = jnp.broadcast_to(cx(gate), (batch, 1, d))
            xc2 = _merge_call(xc2, gate_c, gq, projc, oc_na, yc_ssm, *mw, n_ctx, tm_mg_c)
    return x2.reshape(batch, seq, d)
```

```python
import functools

import jax
import jax.numpy as jnp
import numpy as np
from jax import lax
from jax.experimental import pallas as pl
from jax.experimental.pallas import tpu as pltpu

F32 = jnp.float32
BF16 = jnp.bfloat16

D_MODEL = 2048
GRID_W = 64
RMS_EPS = 1e-6
NEG_INF = -1e30
BRANCH = 512
N_BRANCHES = 4
NA_HEADS = 8
NA_HEAD_DIM = 64
NA_WIN_ROWS = 8
NA_WIN_COLS = 16
NA_GROUP_ROWS = 4
NA_KEY_ROWS = 12
POOL_WINDOWS = (2, 4, 8, 16)
POOL_GROUP_DIM = 128
SSM_GROUPS = 32
SSM_GROUP_DIM = 16
SSM_STATE = 64
IN_TOTAL = 12 * BRANCH + N_BRANCHES * D_MODEL
COL_Q, COL_K, COL_V, COL_NA_Z = 0, 1, 2, 3
COL_POOL_U, COL_POOL_Z = 4, 5
COL_CONV_X, COL_CONV_B, COL_CONV_C, COL_CONV_Z = 6, 7, 8, 9
COL_SSM_U, COL_SSM_Z = 10, 11
COL_MERGE = 12

V7X_LANES = 128
SSM_CHUNK = 8
SSM_LANE_GROUPS = V7X_LANES // SSM_GROUP_DIM
SSM_TILES = BRANCH // V7X_LANES
HALO = 16
VMEM_LIMIT = 56 * 1024 * 1024


def _cparams(sem):
    return pltpu.CompilerParams(dimension_semantics=sem, vmem_limit_bytes=VMEM_LIMIT)


def _const_spec(shape):
    nd = len(shape)
    return pl.BlockSpec(shape, lambda *_: (0,) * nd, pipeline_mode=pl.Buffered(1))


def _layer_spec(stacked_shape, layer):
    nd = len(stacked_shape) - 1
    return pl.BlockSpec((None,) + tuple(stacked_shape[1:]), lambda *_: (layer,) + (0,) * nd,
                        pipeline_mode=pl.Buffered(1))


def _mod_kernel(c_ref, w_ref, b_ref, o_ref):
    c = c_ref[...]
    a = (c * jax.nn.sigmoid(c)).astype(BF16)
    o_ref[0] = jnp.dot(a, w_ref[0].astype(BF16), preferred_element_type=F32) + b_ref[0]


def _mod_call(c8, w_mod, b_mod):
    depth, d, n3 = w_mod.shape
    tn = 512
    return pl.pallas_call(
        _mod_kernel,
        grid=(depth, n3 // tn),
        in_specs=[
            pl.BlockSpec((8, d), lambda l, j: (0, 0)),
            pl.BlockSpec((1, d, tn), lambda l, j: (l, 0, j)),
            pl.BlockSpec((1, 1, tn), lambda l, j: (l, 0, j)),
        ],
        out_specs=pl.BlockSpec((1, 8, tn), lambda l, j: (l, 0, j)),
        out_shape=jax.ShapeDtypeStruct((depth, 8, n3), F32),
        compiler_params=_cparams(("arbitrary", "arbitrary")),
        name="mod",
    )(c8, w_mod, b_mod.reshape(depth, 1, n3))


def _inproj_kernel(x_ref, scale_ref, shift_ref, g_ref, w_ref, o_ref, h_ref):
    @pl.when(pl.program_id(1) == 0)
    def _():
        x = x_ref[...]
        ms = jnp.mean(x * x, axis=-1, keepdims=True)
        gain = g_ref[...] * (1.0 + scale_ref[0])
        h_ref[...] = (x * lax.rsqrt(ms + RMS_EPS) * gain + shift_ref[0]).astype(BF16)

    o_ref[...] = jnp.dot(h_ref[...], w_ref[...], preferred_element_type=F32).astype(o_ref.dtype)


def _inproj_call(x2, scale, shift, g_pre, w_bf, layer, rows_per_seq, tm, tn, col_tiles=None):
    r, d = x2.shape
    n_col_tiles, col_tile_step = col_tiles if col_tiles else (w_bf.shape[2] // tn, 1)
    tiles_per_seq = rows_per_seq // tm
    return pl.pallas_call(
        _inproj_kernel,
        grid=(r // tm, n_col_tiles),
        in_specs=[
            pl.BlockSpec((tm, d), lambda i, j: (i, 0)),
            pl.BlockSpec((1, 1, d), lambda i, j: (i // tiles_per_seq, 0, 0)),
            pl.BlockSpec((1, 1, d), lambda i, j: (i // tiles_per_seq, 0, 0)),
            pl.BlockSpec((1, d), lambda i, j: (0, 0)),
            pl.BlockSpec((None, d, tn), lambda i, j: (layer, 0, j * col_tile_step)),
        ],
        out_specs=pl.BlockSpec((tm, tn), lambda i, j: (i, j)),
        out_shape=jax.ShapeDtypeStruct((r, n_col_tiles * tn), BF16),
        scratch_shapes=[pltpu.VMEM((tm, d), BF16)],
        compiler_params=_cparams(("arbitrary", "arbitrary")),
        name="inproj",
    )(x2, scale, shift, g_pre, w_bf)


def _head_mask(rows):
    lane = lax.broadcasted_iota(jnp.int32, (rows, V7X_LANES), 1)
    return lane < NA_HEAD_DIM


def _na_kernel(q_ref, k_ref, v_ref, kc_ref, vc_ref, t_ref, o_ref, bias_ref, *, grid_rows):
    @pl.when((pl.program_id(0) == 0) & (pl.program_id(1) == 0))
    def _():
        _na_expand_bias(t_ref, bias_ref)

    r0 = pl.program_id(1) * NA_GROUP_ROWS
    nq = NA_GROUP_ROWS * GRID_W
    first_half = _head_mask(nq)
    scale = NA_HEAD_DIM ** -0.5
    nt = (((1,), (1,)), ((), ()))
    gs = jnp.clip(r0 - NA_WIN_ROWS // 2, 0, grid_rows - NA_KEY_ROWS)
    cls = jnp.where(r0 == 0, 0, jnp.where(r0 == grid_rows - NA_GROUP_ROWS, 2, 1))
    k0 = pl.multiple_of(gs * GRID_W, GRID_W)
    for hp in range(NA_HEADS // 2):
        cs = slice(hp * V7X_LANES, (hp + 1) * V7X_LANES)
        qp = q_ref[:, cs] * jnp.asarray(scale, BF16)
        kp = k_ref[pl.ds(k0, NA_KEY_ROWS * GRID_W), cs]
        vp = v_ref[pl.ds(k0, NA_KEY_ROWS * GRID_W), cs]
        kcp = kc_ref[:, cs]
        vcp = vc_ref[:, cs]
        zero = jnp.zeros_like(qp)
        q2 = jnp.concatenate([jnp.where(first_half, qp, zero), jnp.where(first_half, zero, qp)], axis=0)
        s_b = lax.dot_general(q2, kp, nt, preferred_element_type=F32) + bias_ref[cls, hp]
        s_c = lax.dot_general(q2, kcp, nt, preferred_element_type=F32)
        m = jnp.maximum(jnp.max(s_b, axis=-1, keepdims=True), jnp.max(s_c, axis=-1, keepdims=True))
        p_b = jnp.exp(s_b - m)
        p_c = jnp.exp(s_c - m)
        denom = jnp.sum(p_b, axis=-1, keepdims=True) + jnp.sum(p_c, axis=-1, keepdims=True)
        o = (jnp.dot(p_b.astype(BF16), vp, preferred_element_type=F32)
             + jnp.dot(p_c.astype(BF16), vcp, preferred_element_type=F32)) / denom
        o_ref[:, cs] = jnp.where(first_half, o[0:nq], o[nq:]).astype(o_ref.dtype)


def _na_call(proj, projc, bias_tbls, layer, batch, seq, n_ctx):
    grid_rows = seq // GRID_W
    assert grid_rows % NA_GROUP_ROWS == 0 and grid_rows >= NA_KEY_ROWS
    tq = NA_GROUP_ROWS * GRID_W
    qb = seq // tq
    return pl.pallas_call(
        functools.partial(_na_kernel, grid_rows=grid_rows),
        grid=(batch, qb),
        in_specs=[
            pl.BlockSpec((tq, BRANCH), lambda b, r: (b * qb + r, COL_Q)),
            pl.BlockSpec((seq, BRANCH), lambda b, r: (b, COL_K)),
            pl.BlockSpec((seq, BRANCH), lambda b, r: (b, COL_V)),
            pl.BlockSpec((n_ctx, BRANCH), lambda b, r: (b, COL_K)),
            pl.BlockSpec((n_ctx, BRANCH), lambda b, r: (b, COL_V)),
            _layer_spec(bias_tbls.shape, layer),
        ],
        out_specs=pl.BlockSpec((tq, BRANCH), lambda b, r: (b * qb + r, 0)),
        out_shape=jax.ShapeDtypeStruct((batch * seq, BRANCH), BF16),
        scratch_shapes=[pltpu.VMEM((3, NA_HEADS // 2, 2 * tq, NA_KEY_ROWS * GRID_W), F32)],
        compiler_params=_cparams(("arbitrary", "arbitrary")),
        name="na_attn",
    )(proj, proj, proj, projc, projc, bias_tbls)


def _ctx_attn_kernel(q_ref, k_ref, v_ref, o_ref):
    n = q_ref.shape[0]
    first_half = _head_mask(n)
    scale = NA_HEAD_DIM ** -0.5
    nt = (((1,), (1,)), ((), ()))
    for hp in range(NA_HEADS // 2):
        cs = slice(hp * V7X_LANES, (hp + 1) * V7X_LANES)
        qp = q_ref[:, cs] * jnp.asarray(scale, BF16)
        kp = k_ref[:, cs]
        vp = v_ref[:, cs]
        zero = jnp.zeros_like(qp)
        q2 = jnp.concatenate([jnp.where(first_half, qp, zero), jnp.where(first_half, zero, qp)], axis=0)
        s = lax.dot_general(q2, kp, nt, preferred_element_type=F32)
        m = jnp.max(s, axis=-1, keepdims=True)
        p = jnp.exp(s - m)
        denom = jnp.sum(p, axis=-1, keepdims=True)
        o = jnp.dot(p.astype(BF16), vp, preferred_element_type=F32) / denom
        o_ref[:, cs] = jnp.where(first_half, o[0:n], o[n:]).astype(o_ref.dtype)


def _ctx_attn_call(projc, batch, n_ctx):
    return pl.pallas_call(
        _ctx_attn_kernel,
        grid=(batch,),
        in_specs=[
            pl.BlockSpec((n_ctx, BRANCH), lambda b: (b, COL_Q)),
            pl.BlockSpec((n_ctx, BRANCH), lambda b: (b, COL_K)),
            pl.BlockSpec((n_ctx, BRANCH), lambda b: (b, COL_V)),
        ],
        out_specs=pl.BlockSpec((n_ctx, BRANCH), lambda b: (b, 0)),
        out_shape=jax.ShapeDtypeStruct((batch * n_ctx, BRANCH), BF16),
        compiler_params=_cparams(("arbitrary",)),
        name="ctx_attn",
    )(projc, projc, projc)


def _na_bias_table(rpb):
    col = np.arange(GRID_W)
    col_start = np.clip(col - NA_WIN_COLS // 2, 0, GRID_W - NA_WIN_COLS)
    in_win = (col[None, :] >= col_start[:, None]) & (col[None, :] < col_start[:, None] + NA_WIN_COLS)
    dcol = np.clip(col[None, :] - col[:, None] + (NA_WIN_COLS - 1), 0, 2 * NA_WIN_COLS - 2)
    onehot = (dcol[..., None] == np.arange(2 * NA_WIN_COLS - 1)).astype(np.float32)
    t = jnp.einsum('hdc,qwc->hqdw', rpb.astype(F32), onehot, precision=lax.Precision.HIGHEST)
    t = jnp.where(in_win[None, :, None, :], t, NEG_INF)
    t = t.transpose(0, 2, 1, 3)
    return jnp.concatenate([t, t], axis=-1)


def _na_band_rows(cls, i):
    lo = (0, i, NA_KEY_ROWS - NA_WIN_ROWS)[cls]
    off = (NA_WIN_ROWS - 1 - i, NA_WIN_ROWS // 2 - 1 - i, NA_WIN_ROWS // 2 - 1 - lo - i)[cls]
    return lo, off


def _na_expand_bias(t_ref, bias_ref):
    lane = lax.broadcasted_iota(jnp.int32, (GRID_W, V7X_LANES), 1)
    left = lane < GRID_W
    neg = jnp.full((GRID_W, V7X_LANES), NEG_INF, F32)
    for cls in range(3):
        for hp in range(NA_HEADS // 2):
            for e in range(2):
                for i in range(NA_GROUP_ROWS):
                    lo, off = _na_band_rows(cls, i)
                    r0 = (e * NA_GROUP_ROWS + i) * GRID_W
                    for kt in range(NA_KEY_ROWS // 2):
                        halves = []
                        for kw in (2 * kt, 2 * kt + 1):
                            inside = lo <= kw < lo + NA_WIN_ROWS
                            halves.append(t_ref[2 * hp + e, kw + off] if inside else neg)
                        bias_ref[cls, hp, r0:r0 + GRID_W, kt * V7X_LANES:(kt + 1) * V7X_LANES] = (
                            jnp.where(left, halves[0], halves[1]))


def _cmul(ar, ai, br, bi):
    return ar * br - ai * bi, ar * bi + ai * br


def _ssm_weights(a_re, a_im, log_dt, b_re, b_im, c_re, c_im):
    t_len = SSM_CHUNK
    p, hg = SSM_STATE, SSM_GROUP_DIM
    lg = SSM_LANE_GROUPS
    kf, ef, mf, dec = [], [], [], []
    for d in range(2):
        ar, ai = a_re[d].astype(F32), a_im[d].astype(F32)
        dt = jnp.exp(log_dt[d].astype(F32))[:, None]
        mag = jnp.exp(ar * dt)
        abr, abi = mag * jnp.cos(ai * dt), mag * jnp.sin(ai * dt)
        den = ar * ar + ai * ai
        nr, ni = abr - 1.0, abi
        fr = (nr * ar + ni * ai) / den
        fi = (ni * ar - nr * ai) / den
        bbr, bbi = _cmul(fr[..., None], fi[..., None], b_re[d].astype(F32), b_im[d].astype(F32))
        pws = [(jnp.ones_like(abr), jnp.zeros_like(abr))]
        for _ in range(t_len):
            pws.append(_cmul(pws[-1][0], pws[-1][1], abr, abi))
        pr = jnp.stack([x[0] for x in pws])
        pi = jnp.stack([x[1] for x in pws])
        cr, ci = c_re[d].astype(F32), c_im[d].astype(F32)
        mr, mi = _cmul(cr[None], ci[None], pr[:t_len, :, None, :], pi[:t_len, :, None, :])
        kf.append(jnp.einsum('tghp,gpk->tghk', mr, bbr) - jnp.einsum('tghp,gpk->tghk', mi, bbi))
        e_pw = [pws[t_len - 1 - t] if d == 0 else pws[t] for t in range(t_len)]
        per, pei = jnp.stack([x[0] for x in e_pw]), jnp.stack([x[1] for x in e_pw])
        er, ei = _cmul(per[..., None], pei[..., None], bbr[None], bbi[None])
        ef.append((er, ei))
        c_pw = [pws[t + 1] if d == 0 else pws[t_len - t] for t in range(t_len)]
        pcr, pci = jnp.stack([x[0] for x in c_pw]), jnp.stack([x[1] for x in c_pw])
        m2r, m2i = _cmul(cr[None], ci[None], pcr[:, :, None, :], pci[:, :, None, :])
        mf.append((m2r, m2i))
        dec.append((pr[t_len], pi[t_len]))
    kc = [k.reshape(t_len, SSM_TILES, lg, hg, hg).transpose(0, 1, 4, 2, 3).reshape(
        t_len, SSM_TILES, hg, V7X_LANES) for k in kf]
    z = ([kc[1][t_len - 1 - i] for i in range(t_len - 1)] + [kc[0][0] + kc[1][0]]
         + [kc[0][i] for i in range(1, t_len)])
    ktoe = jnp.stack([jnp.stack(z[t_len - 1 - tp:2 * t_len - 1 - tp]) for tp in range(t_len)])
    parts = [ktoe.transpose(2, 0, 3, 1, 4).reshape(SSM_TILES, t_len, hg, t_len * V7X_LANES)]
    for d in range(2):
        for comp in ef[d]:
            parts.append(comp.reshape(t_len, SSM_TILES, lg, p, hg).transpose(1, 0, 4, 2, 3).reshape(
                SSM_TILES, t_len, hg, lg * p))
    c1 = jnp.concatenate(parts, axis=-1)
    rows = []
    for d in range(2):
        m2r, m2i = mf[d]
        for comp in (m2r, -m2i):
            rows.append(comp.reshape(t_len, SSM_TILES, lg, hg, p).transpose(1, 4, 0, 2, 3).reshape(
                SSM_TILES, p, t_len * V7X_LANES))
    cc = jnp.stack(rows, axis=1)
    decay = jnp.stack([x.reshape(SSM_TILES, SSM_TILES, V7X_LANES) for d in range(2) for x in dec[d]], axis=1)
    return c1.astype(BF16), cc.astype(BF16), decay


def _ssm_kernel(uc_ref, u_ref, c1_ref, cc_ref, dec_ref, dsk_ref, yc_ref, y_ref,
                w1_ref, wc_ref, uf_ref, lhs_ref, yi_ref, es_ref, *, n_ctx, seq):
    t_len = SSM_CHUNK
    nc, lc = n_ctx // t_len, seq // t_len
    nj = nc + lc
    kdim = t_len * V7X_LANES
    sw = SSM_LANE_GROUPS * SSM_STATE
    ntile = sw // V7X_LANES
    hg = SSM_GROUP_DIM

    @pl.when(pl.program_id(1) == 0)
    def _():
        lane1 = lax.broadcasted_iota(jnp.int32, (1, kdim + 4 * sw), 1)
        grp1 = jnp.where(lane1 < kdim, (lane1 % V7X_LANES) // hg, ((lane1 - kdim) % sw) // SSM_STATE)
        lanec = lax.broadcasted_iota(jnp.int32, (1, kdim), 1)
        grpc = (lanec % V7X_LANES) // hg
        for t in range(t_len):
            row = c1_ref[0, t]
            for g in range(SSM_LANE_GROUPS):
                r0 = (t * SSM_LANE_GROUPS + g) * hg
                w1_ref[r0:r0 + hg, :] = jnp.where(grp1 == g, row, jnp.zeros_like(row))
        for part in range(4):
            blk = cc_ref[0, part]
            for g in range(SSM_LANE_GROUPS):
                r0 = part * sw + g * SSM_STATE
                wc_ref[r0:r0 + SSM_STATE, :] = jnp.where(grpc == g, blk, jnp.zeros_like(blk))

    uf_ref[0:n_ctx, :] = uc_ref[...].astype(F32)
    uf_ref[n_ctx:n_ctx + seq, :] = u_ref[...].astype(F32)
    for t in range(t_len):
        lhs_ref[:, t * V7X_LANES:(t + 1) * V7X_LANES] = uf_ref[pl.ds(t, nj, stride=t_len), :]
    lhs = lhs_ref[...].astype(BF16)
    yi_ref[...] = jnp.dot(lhs, w1_ref[:, 0:kdim], preferred_element_type=F32) + lhs_ref[...] * dsk_ref[0]
    for c in range(4):
        e = jnp.dot(lhs, w1_ref[:, kdim + c * sw:kdim + (c + 1) * sw], preferred_element_type=F32)
        for k in range(ntile):
            es_ref[c, pl.ds(k, nj, stride=ntile), :] = e[:, k * V7X_LANES:(k + 1) * V7X_LANES]

    afr, afi, abr, abi = dec_ref[0, 0], dec_ref[0, 1], dec_ref[0, 2], dec_ref[0, 3]

    def step(k, st):
        sfr, sfi, sbr, sbi = st
        jb = jnp.where(k < nc, nc - 1 - k, nj - 1 - (k - nc))
        rf = pl.ds(pl.multiple_of(k * ntile, ntile), ntile)
        rb = pl.ds(pl.multiple_of(jb * ntile, ntile), ntile)
        efr, efi = es_ref[0, rf, :], es_ref[1, rf, :]
        ebr, ebi = es_ref[2, rb, :], es_ref[3, rb, :]
        es_ref[0, rf, :] = sfr
        es_ref[1, rf, :] = sfi
        es_ref[2, rb, :] = sbr
        es_ref[3, rb, :] = sbi
        nfr, nfi = _cmul(afr, afi, sfr, sfi)
        nbr, nbi = _cmul(abr, abi, sbr, sbi)
        return nfr + efr, nfi + efi, nbr + ebr, nbi + ebi

    z = jnp.zeros((ntile, V7X_LANES), F32)
    lax.fori_loop(0, nj, step, (z, z, z, z))

    carried = jnp.concatenate(
        [es_ref[c, pl.ds(k, nj, stride=ntile), :] for c in range(4) for k in range(ntile)], axis=1)
    yi_ref[...] += jnp.dot(carried.astype(BF16), wc_ref[...], preferred_element_type=F32)
    for t in range(t_len):
        uf_ref[pl.ds(t, nj, stride=t_len), :] = yi_ref[:, t * V7X_LANES:(t + 1) * V7X_LANES]
    yc_ref[...] = uf_ref[0:n_ctx, :]
    y_ref[...] = uf_ref[n_ctx:n_ctx + seq, :]


def _ssm_call(proj, projc, c1, cc, decay, dsk_tiled, layer, batch, seq, n_ctx, ctx_col):
    t_len = SSM_CHUNK
    nj = (n_ctx + seq) // t_len
    kdim = t_len * V7X_LANES
    sw4 = 4 * SSM_LANE_GROUPS * SSM_STATE
    ntile = sw4 // 4 // V7X_LANES
    col0 = COL_SSM_U * BRANCH // V7X_LANES
    colc = ctx_col // V7X_LANES
    return pl.pallas_call(
        functools.partial(_ssm_kernel, n_ctx=n_ctx, seq=seq),
        grid=(SSM_TILES, batch),
        in_specs=[
            pl.BlockSpec((n_ctx, V7X_LANES), lambda s, b: (b, colc + s)),
            pl.BlockSpec((seq, V7X_LANES), lambda s, b: (b, col0 + s)),
            pl.BlockSpec((None, 1, t_len, SSM_GROUP_DIM, kdim + sw4), lambda s, b: (layer, s, 0, 0, 0)),
            pl.BlockSpec((None, 1, 4, SSM_STATE, kdim), lambda s, b: (layer, s, 0, 0, 0)),
            pl.BlockSpec((None, 1, 4, ntile, V7X_LANES), lambda s, b: (layer, s, 0, 0, 0)),
            pl.BlockSpec((None, 1, 1, kdim), lambda s, b: (layer, s, 0, 0)),
        ],
        out_specs=[
            pl.BlockSpec((n_ctx, V7X_LANES), lambda s, b: (b, s)),
            pl.BlockSpec((seq, V7X_LANES), lambda s, b: (b, s)),
        ],
        out_shape=[
            jax.ShapeDtypeStruct((batch * n_ctx, BRANCH), F32),
            jax.ShapeDtypeStruct((batch * seq, BRANCH), F32),
        ],
        scratch_shapes=[
            pltpu.VMEM((kdim, kdim + sw4), BF16),
            pltpu.VMEM((sw4, kdim), BF16),
            pltpu.VMEM((n_ctx + seq, V7X_LANES), F32),
            pltpu.VMEM((nj, kdim), F32),
            pltpu.VMEM((nj, kdim), F32),
            pltpu.VMEM((4, nj * ntile, V7X_LANES), F32),
        ],
        compiler_params=_cparams(("arbitrary", "arbitrary")),
        name="ssm",
    )(projc, proj, c1, cc, decay, dsk_tiled)


def _merge_kernel(x_ref, gate_ref, gpost_ref,
                  naz_ref, pu_ref, pz_ref, cx_ref, cb_ref, cc_ref, cz_ref, sz_ref,
                  lg0_ref, lg1_ref, lg2_ref, lg3_ref,
                  pu_prev_ref, pu_next_ref, cx_prev_ref, cx_next_ref, cc_prev_ref, cc_next_ref,
                  ona_ref, yssm_ref,
                  poolw_ref, pscale_ref, convw_ref, gluw_ref, bgate_ref, wbr_ref, wo_ref,
                  o_ref, pad_ref, *, tm, rows_per_seq):
    i = pl.program_id(0)
    tiles_per_seq = rows_per_seq // tm
    ti = i % tiles_per_seq
    has_prev = ti > 0
    has_next = ti < tiles_per_seq - 1
    t0 = ti * tm
    pos = t0 + lax.broadcasted_iota(jnp.int32, (tm, 1), 0)

    pad_ref[0:HALO, :] = jnp.where(has_prev, pu_prev_ref[...].astype(F32), 0.0)
    pad_ref[HALO:HALO + tm, :] = pu_ref[...].astype(F32)
    pad_ref[HALO + tm:2 * HALO + tm, :] = jnp.where(has_next, pu_next_ref[...].astype(F32), 0.0)
    mixed = []
    for gi, w in enumerate(POOL_WINDOWS):
        cs = slice(gi * POOL_GROUP_DIM, (gi + 1) * POOL_GROUP_DIM)
        acc = None
        for dlt in range(-(w // 2), w - w // 2):
            v = pad_ref[HALO + dlt:HALO + dlt + tm, cs]
            acc = v if acc is None else acc + v
        lo = jnp.maximum(pos - w // 2, 0)
        hi = jnp.minimum(pos + w - w // 2, rows_per_seq)
        cnt = (hi - lo).astype(F32)
        pooled = acc / cnt - pad_ref[HALO:HALO + tm, cs]
        mixed.append(jnp.dot(pooled.astype(BF16), poolw_ref[gi], preferred_element_type=F32))
    o_pool = jnp.concatenate(mixed, axis=-1) * pscale_ref[...]

    pad_ref[0:HALO, :] = jnp.where(
        has_prev, cc_prev_ref[...].astype(F32) * cx_prev_ref[...].astype(F32), 0.0)
    pad_ref[HALO:HALO + tm, :] = cc_ref[...].astype(F32) * cx_ref[...].astype(F32)
    pad_ref[HALO + tm:2 * HALO + tm, :] = jnp.where(
        has_next, cc_next_ref[...].astype(F32) * cx_next_ref[...].astype(F32), 0.0)
    cw = convw_ref[...]
    conv = (pad_ref[HALO - 1:HALO - 1 + tm, :] * cw[0:1] + pad_ref[HALO:HALO + tm, :] * cw[1:2]
            + pad_ref[HALO + 1:HALO + 1 + tm, :] * cw[2:3])
    o_conv = cb_ref[...].astype(F32) * conv

    g = jax.nn.gelu(yssm_ref[...]).astype(BF16)
    gg = jnp.dot(g, gluw_ref[...], preferred_element_type=F32)
    o_ssm = gg[:, 0:BRANCH] * jax.nn.sigmoid(gg[:, BRANCH:2 * BRANCH])

    outs = (ona_ref[...].astype(F32), o_pool, o_conv, o_ssm)
    zs = (naz_ref, pz_ref, cz_ref, sz_ref)
    lgs = (lg0_ref, lg1_ref, lg2_ref, lg3_ref)
    merged = None
    for bi in range(N_BRANCHES):
        z = zs[bi][...].astype(F32)
        a = (outs[bi] * (z * jax.nn.sigmoid(z))).astype(BF16)
        br = jnp.dot(a, wbr_ref[bi * BRANCH:(bi + 1) * BRANCH, :], preferred_element_type=F32)
        th = jnp.tanh(lgs[bi][...].astype(F32) + bgate_ref[:, bi * D_MODEL:(bi + 1) * D_MODEL])
        term = (1.0 + th) * br
        merged = term if merged is None else merged + term
    y = jnp.dot(merged.astype(BF16), wo_ref[...], preferred_element_type=F32)
    ms = jnp.mean(y * y, axis=-1, keepdims=True)
    yn = y * lax.rsqrt(ms + RMS_EPS) * gpost_ref[...]
    o_ref[...] = x_ref[...] + gate_ref[0] * yn


def _merge_call(x2, gate, g_post, proj, o_na, y_ssm, pool_w, pool_scale, conv_w, glu_w, b_gate,
                w_br, w_o, layer, rows_per_seq, tm):
    r, d = x2.shape
    tiles_per_seq = rows_per_seq // tm
    hb = tm // HALO
    n_halo_blocks = r // HALO

    def col(c):
        return pl.BlockSpec((tm, BRANCH), lambda i: (i, c))

    def lg(c):
        return pl.BlockSpec((tm, D_MODEL), lambda i: (i, COL_MERGE * BRANCH // D_MODEL + c))

    def prev(c):
        return pl.BlockSpec((HALO, BRANCH), lambda i: (jnp.maximum(i * hb - 1, 0), c))

    def nxt(c):
        return pl.BlockSpec((HALO, BRANCH), lambda i: (jnp.minimum((i + 1) * hb, n_halo_blocks - 1), c))

    in_specs = [
        pl.BlockSpec((tm, d), lambda i: (i, 0)),
        pl.BlockSpec((1, 1, d), lambda i: (i // tiles_per_seq, 0, 0)),
        pl.BlockSpec((1, d), lambda i: (0, 0)),
        col(COL_NA_Z), col(COL_POOL_U), col(COL_POOL_Z), col(COL_CONV_X), col(COL_CONV_B),
        col(COL_CONV_C), col(COL_CONV_Z), col(COL_SSM_Z),
        lg(0), lg(1), lg(2), lg(3),
        prev(COL_POOL_U), nxt(COL_POOL_U), prev(COL_CONV_X), nxt(COL_CONV_X),
        prev(COL_CONV_C), nxt(COL_CONV_C),
        pl.BlockSpec((tm, BRANCH), lambda i: (i, 0)),
        pl.BlockSpec((tm, BRANCH), lambda i: (i, 0)),
        _layer_spec(pool_w.shape, layer), _const_spec(pool_scale.shape), _const_spec(conv_w.shape),
        _layer_spec(glu_w.shape, layer), _const_spec(b_gate.shape), _layer_spec(w_br.shape, layer),
        _layer_spec(w_o.shape, layer),
    ]
    args = [x2, gate, g_post] + [proj] * 8 + [proj] * 4 + [proj] * 6 + [
        o_na, y_ssm, pool_w, pool_scale, conv_w, glu_w, b_gate, w_br, w_o]
    return pl.pallas_call(
        functools.partial(_merge_kernel, tm=tm, rows_per_seq=rows_per_seq),
        grid=(r // tm,),
        in_specs=in_specs,
        out_specs=pl.BlockSpec((tm, d), lambda i: (i, 0)),
        out_shape=jax.ShapeDtypeStruct((r, d), F32),
        scratch_shapes=[pltpu.VMEM((tm + 2 * HALO, BRANCH), F32)],
        compiler_params=_cparams(("arbitrary",)),
        name="merge",
    )(*args)


def _tile_rows(rows, pref):
    t = min(rows, pref)
    assert rows % t == 0
    return t


def kernel(x, c, ctx, c_ctx, w_mod, b_mod, g_pre, g_post, w_in, b_gate, na_rpb, pool_w, pool_scale,
           conv_w, ssm_a_re, ssm_a_im, ssm_log_dt, ssm_b_re, ssm_b_im, ssm_c_re, ssm_c_im, ssm_d,
           glu_w, w_br, w_o):
    batch, seq, d = x.shape
    n_ctx = ctx.shape[1]
    depth = w_mod.shape[0]
    assert d == D_MODEL and seq % (GRID_W * NA_WIN_ROWS) == 0 and batch + 1 <= 8
    assert n_ctx % HALO == 0 and w_in.shape[-1] == IN_TOTAL

    c8 = jnp.concatenate([c, c_ctx[None], jnp.zeros((7 - batch, d), F32)], axis=0)
    mod = _mod_call(c8, w_mod, b_mod)

    x2 = x.reshape(batch * seq, d)
    xc2 = ctx.reshape(batch * n_ctx, d)
    tm_in = _tile_rows(seq, 1024)
    tm_in_c = _tile_rows(batch * n_ctx, 1024)
    tm_mg = _tile_rows(seq, 256)
    tm_mg_c = _tile_rows(n_ctx, 256)
    tn_in = 2048

    bias_tbls = jax.vmap(_na_bias_table)(na_rpb)
    c1s, ccs, decays = jax.vmap(_ssm_weights)(ssm_a_re, ssm_a_im, ssm_log_dt, ssm_b_re, ssm_b_im,
                                              ssm_c_re, ssm_c_im)
    dsks = jnp.tile(ssm_d.astype(F32).reshape(depth, SSM_TILES, 1, V7X_LANES), (1, 1, 1, SSM_CHUNK))
    col_scale = jnp.where(jnp.arange(IN_TOTAL) < COL_MERGE * BRANCH, 1.0, 0.5).astype(F32)
    w_in_bf = (w_in * col_scale).astype(BF16)
    pool_w_bf, glu_w_bf = pool_w.astype(BF16), glu_w.astype(BF16)
    w_br_bf, w_o_bf = w_br.astype(BF16), (0.5 * w_o).astype(BF16)

    for i in range(depth):
        with_ctx_out = i < depth - 1
        shift, scale, gate = (mod[i, :, k * d:(k + 1) * d] for k in range(3))
        lat = lambda v: v[:batch].reshape(batch, 1, d)
        cx = lambda v: v[batch:batch + 1].reshape(1, 1, d)
        gp = g_pre[i].reshape(1, d)
        gq = g_post[i].reshape(1, d)

        proj = _inproj_call(x2, lat(scale), lat(shift), gp, w_in_bf, i, seq, tm_in, tn_in)
        if with_ctx_out:
            projc = _inproj_call(xc2, cx(scale), cx(shift), gp, w_in_bf, i, batch * n_ctx, tm_in_c, tn_in)
            ctx_u_col = COL_SSM_U * BRANCH
        else:
            assert COL_V * BRANCH < tn_in and (COL_SSM_U * BRANCH) // tn_in == 2
            projc = _inproj_call(xc2, cx(scale), cx(shift), gp, w_in_bf, i, batch * n_ctx, tm_in_c, tn_in,
                                 col_tiles=(2, 2))
            ctx_u_col = COL_SSM_U * BRANCH - tn_in

        o_na = _na_call(proj, projc, bias_tbls, i, batch, seq, n_ctx)
        yc_ssm, y_ssm = _ssm_call(proj, projc, c1s, ccs, decays, dsks, i, batch, seq, n_ctx, ctx_u_col)

        mw = (pool_w_bf, pool_scale[i].reshape(1, BRANCH), conv_w[i], glu_w_bf,
              0.5 * b_gate[i].reshape(1, N_BRANCHES * d), w_br_bf, w_o_bf, i)
        x2 = _merge_call(x2, lat(gate), gq, proj, o_na, y_ssm, *mw, seq, tm_mg)
        if with_ctx_out:
            oc_na = _ctx_attn_call(projc, batch, n_ctx)
            gate_c = jnp.broadcast_to(cx(gate), (batch, 1, d))
            xc2 = _merge_call(xc2, gate_c, gq, projc, oc_na, yc_ssm, *mw, n_ctx, tm_mg_c)
    return x2.reshape(batch, seq, d)
```

```python
import functools

import jax
import jax.numpy as jnp
import numpy as np
from jax import lax
from jax.experimental import pallas as pl
from jax.experimental.pallas import tpu as pltpu

F32 = jnp.float32
BF16 = jnp.bfloat16

D_MODEL = 2048
GRID_W = 64
RMS_EPS = 1e-6
NEG_INF = -1e30
BRANCH = 512
N_BRANCHES = 4
NA_HEADS = 8
NA_HEAD_DIM = 64
NA_WIN_ROWS = 8
NA_WIN_COLS = 16
NA_GROUP_ROWS = 4
NA_KEY_ROWS = 12
POOL_WINDOWS = (2, 4, 8, 16)
POOL_GROUP_DIM = 128
SSM_GROUPS = 32
SSM_GROUP_DIM = 16
SSM_STATE = 64
IN_TOTAL = 12 * BRANCH + N_BRANCHES * D_MODEL
COL_Q, COL_K, COL_V, COL_NA_Z = 0, 1, 2, 3
COL_POOL_U, COL_POOL_Z = 4, 5
COL_CONV_X, COL_CONV_B, COL_CONV_C, COL_CONV_Z = 6, 7, 8, 9
COL_SSM_U, COL_SSM_Z = 10, 11
COL_MERGE = 12

V7X_LANES = 128
SSM_CHUNK = 8
SSM_LANE_GROUPS = V7X_LANES // SSM_GROUP_DIM
SSM_TILES = BRANCH // V7X_LANES
HALO = 32
assert POOL_WINDOWS == (2, 4, 8, 16) and HALO >= 29 and HALO % 16 == 0
VMEM_LIMIT = 56 * 1024 * 1024


def _cparams(sem):
    return pltpu.CompilerParams(dimension_semantics=sem, vmem_limit_bytes=VMEM_LIMIT)


def _const_spec(shape):
    nd = len(shape)
    return pl.BlockSpec(shape, lambda *_: (0,) * nd, pipeline_mode=pl.Buffered(1))


def _layer_spec(stacked_shape, layer):
    nd = len(stacked_shape) - 1
    return pl.BlockSpec((None,) + tuple(stacked_shape[1:]), lambda *_: (layer,) + (0,) * nd,
                        pipeline_mode=pl.Buffered(1))


def _mod_kernel(c_ref, w_ref, b_ref, o_ref):
    c = c_ref[...]
    a = (c * jax.nn.sigmoid(c)).astype(BF16)
    o_ref[0] = jnp.dot(a, w_ref[0].astype(BF16), preferred_element_type=F32) + b_ref[0]


def _mod_call(c8, w_mod, b_mod):
    depth, d, n3 = w_mod.shape
    tn = 512
    return pl.pallas_call(
        _mod_kernel,
        grid=(depth, n3 // tn),
        in_specs=[
            pl.BlockSpec((8, d), lambda l, j: (0, 0)),
            pl.BlockSpec((1, d, tn), lambda l, j: (l, 0, j)),
            pl.BlockSpec((1, 1, tn), lambda l, j: (l, 0, j)),
        ],
        out_specs=pl.BlockSpec((1, 8, tn), lambda l, j: (l, 0, j)),
        out_shape=jax.ShapeDtypeStruct((depth, 8, n3), F32),
        compiler_params=_cparams(("arbitrary", "arbitrary")),
        name="mod",
    )(c8, w_mod, b_mod.reshape(depth, 1, n3))


def _inproj_kernel(x_ref, scale_ref, shift_ref, g_ref, w_ref, o_ref, h_ref):
    @pl.when(pl.program_id(1) == 0)
    def _():
        x = x_ref[...]
        ms = jnp.mean(x * x, axis=-1, keepdims=True)
        gain = g_ref[...] * (1.0 + scale_ref[0])
        h_ref[...] = (x * lax.rsqrt(ms + RMS_EPS) * gain + shift_ref[0]).astype(BF16)

    o_ref[...] = jnp.dot(h_ref[...], w_ref[...], preferred_element_type=F32).astype(o_ref.dtype)


def _inproj_call(x2, scale, shift, g_pre, w_bf, layer, rows_per_seq, tm, tn, col_tiles=None):
    r, d = x2.shape
    n_col_tiles, col_tile_step = col_tiles if col_tiles else (w_bf.shape[2] // tn, 1)
    tiles_per_seq = rows_per_seq // tm
    return pl.pallas_call(
        _inproj_kernel,
        grid=(r // tm, n_col_tiles),
        in_specs=[
            pl.BlockSpec((tm, d), lambda i, j: (i, 0)),
            pl.BlockSpec((1, 1, d), lambda i, j: (i // tiles_per_seq, 0, 0)),
            pl.BlockSpec((1, 1, d), lambda i, j: (i // tiles_per_seq, 0, 0)),
            pl.BlockSpec((1, d), lambda i, j: (0, 0)),
            pl.BlockSpec((None, d, tn), lambda i, j: (layer, 0, j * col_tile_step)),
        ],
        out_specs=pl.BlockSpec((tm, tn), lambda i, j: (i, j)),
        out_shape=jax.ShapeDtypeStruct((r, n_col_tiles * tn), BF16),
        scratch_shapes=[pltpu.VMEM((tm, d), BF16)],
        compiler_params=_cparams(("arbitrary", "arbitrary")),
        name="inproj",
    )(x2, scale, shift, g_pre, w_bf)


def _head_mask(rows):
    lane = lax.broadcasted_iota(jnp.int32, (rows, V7X_LANES), 1)
    return lane < NA_HEAD_DIM


def _na_kernel(q_ref, k_ref, v_ref, kc_ref, vc_ref, t_ref, o_ref, bias_ref, *, grid_rows):
    @pl.when((pl.program_id(0) == 0) & (pl.program_id(1) == 0))
    def _():
        _na_expand_bias(t_ref, bias_ref)

    r0 = pl.program_id(1) * NA_GROUP_ROWS
    nq = NA_GROUP_ROWS * GRID_W
    first_half = _head_mask(nq)
    scale = NA_HEAD_DIM ** -0.5
    nt = (((1,), (1,)), ((), ()))
    gs = jnp.clip(r0 - NA_WIN_ROWS // 2, 0, grid_rows - NA_KEY_ROWS)
    cls = jnp.where(r0 == 0, 0, jnp.where(r0 == grid_rows - NA_GROUP_ROWS, 2, 1))
    k0 = pl.multiple_of(gs * GRID_W, GRID_W)
    for hp in range(NA_HEADS // 2):
        cs = slice(hp * V7X_LANES, (hp + 1) * V7X_LANES)
        qp = q_ref[:, cs] * jnp.asarray(scale, BF16)
        kp = k_ref[pl.ds(k0, NA_KEY_ROWS * GRID_W), cs]
        vp = v_ref[pl.ds(k0, NA_KEY_ROWS * GRID_W), cs]
        kcp = kc_ref[:, cs]
        vcp = vc_ref[:, cs]
        zero = jnp.zeros_like(qp)
        q2 = jnp.concatenate([jnp.where(first_half, qp, zero), jnp.where(first_half, zero, qp)], axis=0)
        s_b = lax.dot_general(q2, kp, nt, preferred_element_type=F32) + bias_ref[cls, hp]
        s_c = lax.dot_general(q2, kcp, nt, preferred_element_type=F32)
        m = jnp.maximum(jnp.max(s_b, axis=-1, keepdims=True), jnp.max(s_c, axis=-1, keepdims=True))
        p_b = jnp.exp(s_b - m)
        p_c = jnp.exp(s_c - m)
        denom = jnp.sum(p_b, axis=-1, keepdims=True) + jnp.sum(p_c, axis=-1, keepdims=True)
        o = (jnp.dot(p_b.astype(BF16), vp, preferred_element_type=F32)
             + jnp.dot(p_c.astype(BF16), vcp, preferred_element_type=F32)) / denom
        o_ref[:, cs] = jnp.where(first_half, o[0:nq], o[nq:]).astype(o_ref.dtype)


def _na_call(proj, projc, bias_tbls, layer, batch, seq, n_ctx):
    grid_rows = seq // GRID_W
    assert grid_rows % NA_GROUP_ROWS == 0 and grid_rows >= NA_KEY_ROWS
    tq = NA_GROUP_ROWS * GRID_W
    qb = seq // tq
    return pl.pallas_call(
        functools.partial(_na_kernel, grid_rows=grid_rows),
        grid=(batch, qb),
        in_specs=[
            pl.BlockSpec((tq, BRANCH), lambda b, r: (b * qb + r, COL_Q)),
            pl.BlockSpec((seq, BRANCH), lambda b, r: (b, COL_K)),
            pl.BlockSpec((seq, BRANCH), lambda b, r: (b, COL_V)),
            pl.BlockSpec((n_ctx, BRANCH), lambda b, r: (b, COL_K)),
            pl.BlockSpec((n_ctx, BRANCH), lambda b, r: (b, COL_V)),
            _layer_spec(bias_tbls.shape, layer),
        ],
        out_specs=pl.BlockSpec((tq, BRANCH), lambda b, r: (b * qb + r, 0)),
        out_shape=jax.ShapeDtypeStruct((batch * seq, BRANCH), BF16),
        scratch_shapes=[pltpu.VMEM((3, NA_HEADS // 2, 2 * tq, NA_KEY_ROWS * GRID_W), F32)],
        compiler_params=_cparams(("arbitrary", "arbitrary")),
        name="na_attn",
    )(proj, proj, proj, projc, projc, bias_tbls)


def _ctx_attn_kernel(q_ref, k_ref, v_ref, o_ref):
    n = q_ref.shape[0]
    first_half = _head_mask(n)
    scale = NA_HEAD_DIM ** -0.5
    nt = (((1,), (1,)), ((), ()))
    for hp in range(NA_HEADS // 2):
        cs = slice(hp * V7X_LANES, (hp + 1) * V7X_LANES)
        qp = q_ref[:, cs] * jnp.asarray(scale, BF16)
        kp = k_ref[:, cs]
        vp = v_ref[:, cs]
        zero = jnp.zeros_like(qp)
        q2 = jnp.concatenate([jnp.where(first_half, qp, zero), jnp.where(first_half, zero, qp)], axis=0)
        s = lax.dot_general(q2, kp, nt, preferred_element_type=F32)
        m = jnp.max(s, axis=-1, keepdims=True)
        p = jnp.exp(s - m)
        denom = jnp.sum(p, axis=-1, keepdims=True)
        o = jnp.dot(p.astype(BF16), vp, preferred_element_type=F32) / denom
        o_ref[:, cs] = jnp.where(first_half, o[0:n], o[n:]).astype(o_ref.dtype)


def _ctx_attn_call(projc, batch, n_ctx):
    return pl.pallas_call(
        _ctx_attn_kernel,
        grid=(batch,),
        in_specs=[
            pl.BlockSpec((n_ctx, BRANCH), lambda b: (b, COL_Q)),
            pl.BlockSpec((n_ctx, BRANCH), lambda b: (b, COL_K)),
            pl.BlockSpec((n_ctx, BRANCH), lambda b: (b, COL_V)),
        ],
        out_specs=pl.BlockSpec((n_ctx, BRANCH), lambda b: (b, 0)),
        out_shape=jax.ShapeDtypeStruct((batch * n_ctx, BRANCH), BF16),
        compiler_params=_cparams(("arbitrary",)),
        name="ctx_attn",
    )(projc, projc, projc)


def _na_bias_table(rpb):
    col = np.arange(GRID_W)
    col_start = np.clip(col - NA_WIN_COLS // 2, 0, GRID_W - NA_WIN_COLS)
    in_win = (col[None, :] >= col_start[:, None]) & (col[None, :] < col_start[:, None] + NA_WIN_COLS)
    dcol = np.clip(col[None, :] - col[:, None] + (NA_WIN_COLS - 1), 0, 2 * NA_WIN_COLS - 2)
    onehot = (dcol[..., None] == np.arange(2 * NA_WIN_COLS - 1)).astype(np.float32)
    t = jnp.einsum('hdc,qwc->hqdw', rpb.astype(F32), onehot, precision=lax.Precision.HIGHEST)
    t = jnp.where(in_win[None, :, None, :], t, NEG_INF)
    t = t.transpose(0, 2, 1, 3)
    return jnp.concatenate([t, t], axis=-1)


def _na_band_rows(cls, i):
    lo = (0, i, NA_KEY_ROWS - NA_WIN_ROWS)[cls]
    off = (NA_WIN_ROWS - 1 - i, NA_WIN_ROWS // 2 - 1 - i, NA_WIN_ROWS // 2 - 1 - lo - i)[cls]
    return lo, off


def _na_expand_bias(t_ref, bias_ref):
    lane = lax.broadcasted_iota(jnp.int32, (GRID_W, V7X_LANES), 1)
    left = lane < GRID_W
    neg = jnp.full((GRID_W, V7X_LANES), NEG_INF, F32)
    for cls in range(3):
        for hp in range(NA_HEADS // 2):
            for e in range(2):
                for i in range(NA_GROUP_ROWS):
                    lo, off = _na_band_rows(cls, i)
                    r0 = (e * NA_GROUP_ROWS + i) * GRID_W
                    for kt in range(NA_KEY_ROWS // 2):
                        halves = []
                        for kw in (2 * kt, 2 * kt + 1):
                            inside = lo <= kw < lo + NA_WIN_ROWS
                            halves.append(t_ref[2 * hp + e, kw + off] if inside else neg)
                        bias_ref[cls, hp, r0:r0 + GRID_W, kt * V7X_LANES:(kt + 1) * V7X_LANES] = (
                            jnp.where(left, halves[0], halves[1]))


def _cmul(ar, ai, br, bi):
    return ar * br - ai * bi, ar * bi + ai * br


def _ssm_weights(a_re, a_im, log_dt, b_re, b_im, c_re, c_im):
    t_len = SSM_CHUNK
    p, hg = SSM_STATE, SSM_GROUP_DIM
    lg = SSM_LANE_GROUPS
    kf, ef, mf, dec = [], [], [], []
    for d in range(2):
        ar, ai = a_re[d].astype(F32), a_im[d].astype(F32)
        dt = jnp.exp(log_dt[d].astype(F32))[:, None]
        mag = jnp.exp(ar * dt)
        abr, abi = mag * jnp.cos(ai * dt), mag * jnp.sin(ai * dt)
        den = ar * ar + ai * ai
        nr, ni = abr - 1.0, abi
        fr = (nr * ar + ni * ai) / den
        fi = (ni * ar - nr * ai) / den
        bbr, bbi = _cmul(fr[..., None], fi[..., None], b_re[d].astype(F32), b_im[d].astype(F32))
        pws = [(jnp.ones_like(abr), jnp.zeros_like(abr))]
        for _ in range(t_len):
            pws.append(_cmul(pws[-1][0], pws[-1][1], abr, abi))
        pr = jnp.stack([x[0] for x in pws])
        pi = jnp.stack([x[1] for x in pws])
        cr, ci = c_re[d].astype(F32), c_im[d].astype(F32)
        mr, mi = _cmul(cr[None], ci[None], pr[:t_len, :, None, :], pi[:t_len, :, None, :])
        kf.append(jnp.einsum('tghp,gpk->tghk', mr, bbr) - jnp.einsum('tghp,gpk->tghk', mi, bbi))
        e_pw = [pws[t_len - 1 - t] if d == 0 else pws[t] for t in range(t_len)]
        per, pei = jnp.stack([x[0] for x in e_pw]), jnp.stack([x[1] for x in e_pw])
        er, ei = _cmul(per[..., None], pei[..., None], bbr[None], bbi[None])
        ef.append((er, ei))
        c_pw = [pws[t + 1] if d == 0 else pws[t_len - t] for t in range(t_len)]
        pcr, pci = jnp.stack([x[0] for x in c_pw]), jnp.stack([x[1] for x in c_pw])
        m2r, m2i = _cmul(cr[None], ci[None], pcr[:, :, None, :], pci[:, :, None, :])
        mf.append((m2r, m2i))
        dec.append((pr[t_len], pi[t_len]))
    kc = [k.reshape(t_len, SSM_TILES, lg, hg, hg).transpose(0, 1, 4, 2, 3).reshape(
        t_len, SSM_TILES, hg, V7X_LANES) for k in kf]
    z = ([kc[1][t_len - 1 - i] for i in range(t_len - 1)] + [kc[0][0] + kc[1][0]]
         + [kc[0][i] for i in range(1, t_len)])
    ktoe = jnp.stack([jnp.stack(z[t_len - 1 - tp:2 * t_len - 1 - tp]) for tp in range(t_len)])
    parts = [ktoe.transpose(2, 0, 3, 1, 4).reshape(SSM_TILES, t_len, hg, t_len * V7X_LANES)]
    for d in range(2):
        for comp in ef[d]:
            parts.append(comp.reshape(t_len, SSM_TILES, lg, p, hg).transpose(1, 0, 4, 2, 3).reshape(
                SSM_TILES, t_len, hg, lg * p))
    c1 = jnp.concatenate(parts, axis=-1)
    rows = []
    for d in range(2):
        m2r, m2i = mf[d]
        for comp in (m2r, -m2i):
            rows.append(comp.reshape(t_len, SSM_TILES, lg, hg, p).transpose(1, 4, 0, 2, 3).reshape(
                SSM_TILES, p, t_len * V7X_LANES))
    cc = jnp.stack(rows, axis=1)
    decay = jnp.stack([x.reshape(SSM_TILES, SSM_TILES, V7X_LANES) for d in range(2) for x in dec[d]], axis=1)
    return c1.astype(BF16), cc.astype(BF16), decay


def _ssm_kernel(uc_ref, u_ref, c1_ref, cc_ref, dec_ref, dsk_ref, yc_ref, y_ref,
                w1_ref, wc_ref, uf_ref, lhs_ref, yi_ref, es_ref, *, n_ctx, seq):
    t_len = SSM_CHUNK
    nc, lc = n_ctx // t_len, seq // t_len
    nj = nc + lc
    kdim = t_len * V7X_LANES
    sw = SSM_LANE_GROUPS * SSM_STATE
    ntile = sw // V7X_LANES
    hg = SSM_GROUP_DIM

    @pl.when(pl.program_id(1) == 0)
    def _():
        lane1 = lax.broadcasted_iota(jnp.int32, (1, kdim + 4 * sw), 1)
        grp1 = jnp.where(lane1 < kdim, (lane1 % V7X_LANES) // hg, ((lane1 - kdim) % sw) // SSM_STATE)
        lanec = lax.broadcasted_iota(jnp.int32, (1, kdim), 1)
        grpc = (lanec % V7X_LANES) // hg
        for t in range(t_len):
            row = c1_ref[0, t]
            for g in range(SSM_LANE_GROUPS):
                r0 = (t * SSM_LANE_GROUPS + g) * hg
                w1_ref[r0:r0 + hg, :] = jnp.where(grp1 == g, row, jnp.zeros_like(row))
        for part in range(4):
            blk = cc_ref[0, part]
            for g in range(SSM_LANE_GROUPS):
                r0 = part * sw + g * SSM_STATE
                wc_ref[r0:r0 + SSM_STATE, :] = jnp.where(grpc == g, blk, jnp.zeros_like(blk))

    uf_ref[0:n_ctx, :] = uc_ref[...].astype(F32)
    uf_ref[n_ctx:n_ctx + seq, :] = u_ref[...].astype(F32)
    for t in range(t_len):
        lhs_ref[:, t * V7X_LANES:(t + 1) * V7X_LANES] = uf_ref[pl.ds(t, nj, stride=t_len), :]
    lhs = lhs_ref[...].astype(BF16)
    yi_ref[...] = jnp.dot(lhs, w1_ref[:, 0:kdim], preferred_element_type=F32) + lhs_ref[...] * dsk_ref[0]
    for c in range(4):
        e = jnp.dot(lhs, w1_ref[:, kdim + c * sw:kdim + (c + 1) * sw], preferred_element_type=F32)
        for k in range(ntile):
            es_ref[c, pl.ds(k, nj, stride=ntile), :] = e[:, k * V7X_LANES:(k + 1) * V7X_LANES]

    afr, afi, abr, abi = dec_ref[0, 0], dec_ref[0, 1], dec_ref[0, 2], dec_ref[0, 3]

    def step(k, st):
        sfr, sfi, sbr, sbi = st
        jb = jnp.where(k < nc, nc - 1 - k, nj - 1 - (k - nc))
        rf = pl.ds(pl.multiple_of(k * ntile, ntile), ntile)
        rb = pl.ds(pl.multiple_of(jb * ntile, ntile), ntile)
        efr, efi = es_ref[0, rf, :], es_ref[1, rf, :]
        ebr, ebi = es_ref[2, rb, :], es_ref[3, rb, :]
        es_ref[0, rf, :] = sfr
        es_ref[1, rf, :] = sfi
        es_ref[2, rb, :] = sbr
        es_ref[3, rb, :] = sbi
        nfr, nfi = _cmul(afr, afi, sfr, sfi)
        nbr, nbi = _cmul(abr, abi, sbr, sbi)
        return nfr + efr, nfi + efi, nbr + ebr, nbi + ebi

    z = jnp.zeros((ntile, V7X_LANES), F32)
    lax.fori_loop(0, nj, step, (z, z, z, z))

    carried = jnp.concatenate(
        [es_ref[c, pl.ds(k, nj, stride=ntile), :] for c in range(4) for k in range(ntile)], axis=1)
    yi_ref[...] += jnp.dot(carried.astype(BF16), wc_ref[...], preferred_element_type=F32)
    for t in range(t_len):
        uf_ref[pl.ds(t, nj, stride=t_len), :] = yi_ref[:, t * V7X_LANES:(t + 1) * V7X_LANES]
    yc_ref[...] = uf_ref[0:n_ctx, :]
    y_ref[...] = uf_ref[n_ctx:n_ctx + seq, :]


def _ssm_call(proj, projc, c1, cc, decay, dsk_tiled, layer, batch, seq, n_ctx, ctx_col):
    t_len = SSM_CHUNK
    nj = (n_ctx + seq) // t_len
    kdim = t_len * V7X_LANES
    sw4 = 4 * SSM_LANE_GROUPS * SSM_STATE
    ntile = sw4 // 4 // V7X_LANES
    col0 = COL_SSM_U * BRANCH // V7X_LANES
    colc = ctx_col // V7X_LANES
    return pl.pallas_call(
        functools.partial(_ssm_kernel, n_ctx=n_ctx, seq=seq),
        grid=(SSM_TILES, batch),
        in_specs=[
            pl.BlockSpec((n_ctx, V7X_LANES), lambda s, b: (b, colc + s)),
            pl.BlockSpec((seq, V7X_LANES), lambda s, b: (b, col0 + s)),
            pl.BlockSpec((None, 1, t_len, SSM_GROUP_DIM, kdim + sw4), lambda s, b: (layer, s, 0, 0, 0)),
            pl.BlockSpec((None, 1, 4, SSM_STATE, kdim), lambda s, b: (layer, s, 0, 0, 0)),
            pl.BlockSpec((None, 1, 4, ntile, V7X_LANES), lambda s, b: (layer, s, 0, 0, 0)),
            pl.BlockSpec((None, 1, 1, kdim), lambda s, b: (layer, s, 0, 0)),
        ],
        out_specs=[
            pl.BlockSpec((n_ctx, V7X_LANES), lambda s, b: (b, s)),
            pl.BlockSpec((seq, V7X_LANES), lambda s, b: (b, s)),
        ],
        out_shape=[
            jax.ShapeDtypeStruct((batch * n_ctx, BRANCH), F32),
            jax.ShapeDtypeStruct((batch * seq, BRANCH), F32),
        ],
        scratch_shapes=[
            pltpu.VMEM((kdim, kdim + sw4), BF16),
            pltpu.VMEM((sw4, kdim), BF16),
            pltpu.VMEM((n_ctx + seq, V7X_LANES), F32),
            pltpu.VMEM((nj, kdim), F32),
            pltpu.VMEM((nj, kdim), F32),
            pltpu.VMEM((4, nj * ntile, V7X_LANES), F32),
        ],
        compiler_params=_cparams(("arbitrary", "arbitrary")),
        name="ssm",
    )(projc, proj, c1, cc, decay, dsk_tiled)


def _merge_kernel(x_ref, gate_ref, gpost_ref,
                  naz_ref, pu_ref, pz_ref, cx_ref, cb_ref, cc_ref, cz_ref, sz_ref,
                  lg0_ref, lg1_ref, lg2_ref, lg3_ref,
                  pu_prev_ref, pu_next_ref, cx_prev_ref, cx_next_ref, cc_prev_ref, cc_next_ref,
                  ona_ref, yssm_ref,
                  poolw_ref, pscale_ref, convw_ref, gluw_ref, bgate_ref, wbr_ref, wo_ref,
                  o_ref, pad_ref, s2_ref, s4_ref, *, tm, rows_per_seq):
    i = pl.program_id(0)
    tiles_per_seq = rows_per_seq // tm
    ti = i % tiles_per_seq
    has_prev = ti > 0
    has_next = ti < tiles_per_seq - 1
    t0 = ti * tm
    pos = t0 + lax.broadcasted_iota(jnp.int32, (tm, 1), 0)

    pad_ref[0:HALO, :] = jnp.where(has_prev, pu_prev_ref[...].astype(F32), 0.0)
    pad_ref[HALO:HALO + tm, :] = pu_ref[...].astype(F32)
    pad_ref[HALO + tm:2 * HALO + tm, :] = jnp.where(has_next, pu_next_ref[...].astype(F32), 0.0)
    base = HALO - max(POOL_WINDOWS) // 2
    n8, n4, n2 = tm + 16, tm + 24, tm + 32
    mixed = []
    for gi, w in enumerate(POOL_WINDOWS):
        cs = slice(gi * POOL_GROUP_DIM, (gi + 1) * POOL_GROUP_DIM)
        half = pad_ref
        if w >= 4:
            s2_ref[base:base + n2, cs] = pad_ref[base:base + n2, cs] + pad_ref[base + 1:base + 1 + n2, cs]
            half = s2_ref
        if w >= 8:
            s4_ref[base:base + n4, cs] = s2_ref[base:base + n4, cs] + s2_ref[base + 2:base + 2 + n4, cs]
            half = s4_ref
        if w >= 16:
            s2_ref[base:base + n8, cs] = s4_ref[base:base + n8, cs] + s4_ref[base + 4:base + 4 + n8, cs]
            half = s2_ref
        acc = half[HALO - w // 2:HALO - w // 2 + tm, cs] + half[HALO:HALO + tm, cs]
        lo = jnp.maximum(pos - w // 2, 0)
        hi = jnp.minimum(pos + w - w // 2, rows_per_seq)
        cnt = (hi - lo).astype(F32)
        pooled = acc / cnt - pad_ref[HALO:HALO + tm, cs]
        mixed.append(jnp.dot(pooled.astype(BF16), poolw_ref[gi], preferred_element_type=F32))
    o_pool = jnp.concatenate(mixed, axis=-1) * pscale_ref[...]

    pad_ref[0:HALO, :] = jnp.where(
        has_prev, cc_prev_ref[...].astype(F32) * cx_prev_ref[...].astype(F32), 0.0)
    pad_ref[HALO:HALO + tm, :] = cc_ref[...].astype(F32) * cx_ref[...].astype(F32)
    pad_ref[HALO + tm:2 * HALO + tm, :] = jnp.where(
        has_next, cc_next_ref[...].astype(F32) * cx_next_ref[...].astype(F32), 0.0)
    cw = convw_ref[...]
    conv = (pad_ref[HALO - 1:HALO - 1 + tm, :] * cw[0:1] + pad_ref[HALO:HALO + tm, :] * cw[1:2]
            + pad_ref[HALO + 1:HALO + 1 + tm, :] * cw[2:3])
    o_conv = cb_ref[...].astype(F32) * conv

    g = jax.nn.gelu(yssm_ref[...]).astype(BF16)
    gg = jnp.dot(g, gluw_ref[...], preferred_element_type=F32)
    ga = 0.5 * gg[:, 0:BRANCH]
    o_ssm = ga + ga * jnp.tanh(0.5 * gg[:, BRANCH:2 * BRANCH])

    outs = (ona_ref[...].astype(F32), o_pool, o_conv, o_ssm)
    zs = (naz_ref, pz_ref, cz_ref, sz_ref)
    lgs = (lg0_ref, lg1_ref, lg2_ref, lg3_ref)
    merged = None
    for bi in range(N_BRANCHES):
        hz = 0.5 * zs[bi][...].astype(F32)
        a = (outs[bi] * (hz + hz * jnp.tanh(hz))).astype(BF16)
        br = jnp.dot(a, wbr_ref[bi * BRANCH:(bi + 1) * BRANCH, :], preferred_element_type=F32)
        th = jnp.tanh(lgs[bi][...].astype(F32) + bgate_ref[:, bi * D_MODEL:(bi + 1) * D_MODEL])
        term = (1.0 + th) * br
        merged = term if merged is None else merged + term
    y = jnp.dot(merged.astype(BF16), wo_ref[...], preferred_element_type=F32)
    ms = jnp.mean(y * y, axis=-1, keepdims=True)
    gain = gpost_ref[...] * gate_ref[0]
    o_ref[...] = x_ref[...] + y * lax.rsqrt(ms + RMS_EPS) * gain


def _merge_call(x2, gate, g_post, proj, o_na, y_ssm, pool_w, pool_scale, conv_w, glu_w, b_gate,
                w_br, w_o, layer, rows_per_seq, tm):
    r, d = x2.shape
    tiles_per_seq = rows_per_seq // tm
    hb = tm // HALO
    n_halo_blocks = r // HALO

    def col(c):
        return pl.BlockSpec((tm, BRANCH), lambda i: (i, c))

    def lg(c):
        return pl.BlockSpec((tm, D_MODEL), lambda i: (i, COL_MERGE * BRANCH // D_MODEL + c))

    def prev(c):
        return pl.BlockSpec((HALO, BRANCH), lambda i: (jnp.maximum(i * hb - 1, 0), c))

    def nxt(c):
        return pl.BlockSpec((HALO, BRANCH), lambda i: (jnp.minimum((i + 1) * hb, n_halo_blocks - 1), c))

    in_specs = [
        pl.BlockSpec((tm, d), lambda i: (i, 0)),
        pl.BlockSpec((1, 1, d), lambda i: (i // tiles_per_seq, 0, 0)),
        pl.BlockSpec((1, d), lambda i: (0, 0)),
        col(COL_NA_Z), col(COL_POOL_U), col(COL_POOL_Z), col(COL_CONV_X), col(COL_CONV_B),
        col(COL_CONV_C), col(COL_CONV_Z), col(COL_SSM_Z),
        lg(0), lg(1), lg(2), lg(3),
        prev(COL_POOL_U), nxt(COL_POOL_U), prev(COL_CONV_X), nxt(COL_CONV_X),
        prev(COL_CONV_C), nxt(COL_CONV_C),
        pl.BlockSpec((tm, BRANCH), lambda i: (i, 0)),
        pl.BlockSpec((tm, BRANCH), lambda i: (i, 0)),
        _layer_spec(pool_w.shape, layer), _const_spec(pool_scale.shape), _const_spec(conv_w.shape),
        _layer_spec(glu_w.shape, layer), _const_spec(b_gate.shape), _layer_spec(w_br.shape, layer),
        _layer_spec(w_o.shape, layer),
    ]
    args = [x2, gate, g_post] + [proj] * 8 + [proj] * 4 + [proj] * 6 + [
        o_na, y_ssm, pool_w, pool_scale, conv_w, glu_w, b_gate, w_br, w_o]
    return pl.pallas_call(
        functools.partial(_merge_kernel, tm=tm, rows_per_seq=rows_per_seq),
        grid=(r // tm,),
        in_specs=in_specs,
        out_specs=pl.BlockSpec((tm, d), lambda i: (i, 0)),
        out_shape=jax.ShapeDtypeStruct((r, d), F32),
        scratch_shapes=[pltpu.VMEM((tm + 2 * HALO, BRANCH), F32)] * 3,
        compiler_params=_cparams(("arbitrary",)),
        name="merge",
    )(*args)


def _tile_rows(rows, pref):
    t = min(rows, pref)
    assert rows % t == 0
    return t


def kernel(x, c, ctx, c_ctx, w_mod, b_mod, g_pre, g_post, w_in, b_gate, na_rpb, pool_w, pool_scale,
           conv_w, ssm_a_re, ssm_a_im, ssm_log_dt, ssm_b_re, ssm_b_im, ssm_c_re, ssm_c_im, ssm_d,
           glu_w, w_br, w_o):
    batch, seq, d = x.shape
    n_ctx = ctx.shape[1]
    depth = w_mod.shape[0]
    assert d == D_MODEL and seq % (GRID_W * NA_WIN_ROWS) == 0 and batch + 1 <= 8
    assert n_ctx % HALO == 0 and w_in.shape[-1] == IN_TOTAL

    c8 = jnp.concatenate([c, c_ctx[None], jnp.zeros((7 - batch, d), F32)], axis=0)
    mod = _mod_call(c8, w_mod, b_mod)

    x2 = x.reshape(batch * seq, d)
    xc2 = ctx.reshape(batch * n_ctx, d)
    tm_in = _tile_rows(seq, 1024)
    tm_in_c = _tile_rows(batch * n_ctx, 1024)
    tm_mg = _tile_rows(seq, 256)
    tm_mg_c = _tile_rows(n_ctx, 256)
    tn_in = 2048

    bias_tbls = jax.vmap(_na_bias_table)(na_rpb)
    c1s, ccs, decays = jax.vmap(_ssm_weights)(ssm_a_re, ssm_a_im, ssm_log_dt, ssm_b_re, ssm_b_im,
                                              ssm_c_re, ssm_c_im)
    dsks = jnp.tile(ssm_d.astype(F32).reshape(depth, SSM_TILES, 1, V7X_LANES), (1, 1, 1, SSM_CHUNK))
    col_scale = jnp.where(jnp.arange(IN_TOTAL) < COL_MERGE * BRANCH, 1.0, 0.5).astype(F32)
    w_in_bf = (w_in * col_scale).astype(BF16)
    pool_w_bf, glu_w_bf = pool_w.astype(BF16), glu_w.astype(BF16)
    w_br_bf, w_o_bf = w_br.astype(BF16), (0.5 * w_o).astype(BF16)

    for i in range(depth):
        with_ctx_out = i < depth - 1
        shift, scale, gate = (mod[i, :, k * d:(k + 1) * d] for k in range(3))
        lat = lambda v: v[:batch].reshape(batch, 1, d)
        cx = lambda v: v[batch:batch + 1].reshape(1, 1, d)
        gp = g_pre[i].reshape(1, d)
        gq = g_post[i].reshape(1, d)

        proj = _inproj_call(x2, lat(scale), lat(shift), gp, w_in_bf, i, seq, tm_in, tn_in)
        if with_ctx_out:
            projc = _inproj_call(xc2, cx(scale), cx(shift), gp, w_in_bf, i, batch * n_ctx, tm_in_c, tn_in)
            ctx_u_col = COL_SSM_U * BRANCH
        else:
            assert COL_V * BRANCH < tn_in and (COL_SSM_U * BRANCH) // tn_in == 2
            projc = _inproj_call(xc2, cx(scale), cx(shift), gp, w_in_bf, i, batch * n_ctx, tm_in_c, tn_in,
                                 col_tiles=(2, 2))
            ctx_u_col = COL_SSM_U * BRANCH - tn_in

        o_na = _na_call(proj, projc, bias_tbls, i, batch, seq, n_ctx)
        yc_ssm, y_ssm = _ssm_call(proj, projc, c1s, ccs, decays, dsks, i, batch, seq, n_ctx, ctx_u_col)

        mw = (pool_w_bf, pool_scale[i].reshape(1, BRANCH), conv_w[i], glu_w_bf,
              0.5 * b_gate[i].reshape(1, N_BRANCHES * d), w_br_bf, w_o_bf, i)
        x2 = _merge_call(x2, lat(gate), gq, proj, o_na, y_ssm, *mw, seq, tm_mg)
        if with_ctx_out:
            oc_na = _ctx_attn_call(projc, batch, n_ctx)
            gate_c = jnp.broadcast_to(cx(gate), (batch, 1, d))
            xc2 = _merge_call(xc2, gate_c, gq, projc, oc_na, yc_ssm, *mw, n_ctx, tm_mg_c)
    return x2.reshape(batch, seq, d)
```

```python
import functools

import jax
import jax.numpy as jnp
import numpy as np
from jax import lax
from jax.experimental import pallas as pl
from jax.experimental.pallas import tpu as pltpu

F32 = jnp.float32
BF16 = jnp.bfloat16

D_MODEL = 2048
GRID_W = 64
RMS_EPS = 1e-6
NEG_INF = -1e30
BRANCH = 512
N_BRANCHES = 4
NA_HEADS = 8
NA_HEAD_DIM = 64
NA_WIN_ROWS = 8
NA_WIN_COLS = 16
NA_GROUP_ROWS = 4
NA_KEY_ROWS = 12
POOL_WINDOWS = (2, 4, 8, 16)
POOL_GROUP_DIM = 128
SSM_GROUPS = 32
SSM_GROUP_DIM = 16
SSM_STATE = 64
IN_TOTAL = 12 * BRANCH + N_BRANCHES * D_MODEL
COL_Q, COL_K, COL_V, COL_NA_Z = 0, 1, 2, 3
COL_POOL_U, COL_POOL_Z = 4, 5
COL_CONV_X, COL_CONV_B, COL_CONV_C, COL_CONV_Z = 6, 7, 8, 9
COL_SSM_U, COL_SSM_Z = 10, 11
COL_MERGE = 12

V7X_LANES = 128
SSM_CHUNK = 8
SSM_LANE_GROUPS = V7X_LANES // SSM_GROUP_DIM
SSM_TILES = BRANCH // V7X_LANES
SSM_BATCH_PER_STEP = 2
HALO = 32
assert POOL_WINDOWS == (2, 4, 8, 16) and HALO >= 29 and HALO % 16 == 0
VMEM_LIMIT = 56 * 1024 * 1024
INPROJ_ROW_TILE = 1024
INPROJ_COL_TILE = 2048
MERGE_ROW_TILE = 256


def _cparams(sem):
    return pltpu.CompilerParams(dimension_semantics=sem, vmem_limit_bytes=VMEM_LIMIT)


def _const_spec(shape):
    nd = len(shape)
    return pl.BlockSpec(shape, lambda *_: (0,) * nd, pipeline_mode=pl.Buffered(1))


def _layer_spec(stacked_shape, layer):
    nd = len(stacked_shape) - 1
    return pl.BlockSpec((None,) + tuple(stacked_shape[1:]), lambda *_: (layer,) + (0,) * nd,
                        pipeline_mode=pl.Buffered(1))


def _mod_kernel(c_ref, w_ref, b_ref, o_ref):
    c = c_ref[...]
    a = (c * jax.nn.sigmoid(c)).astype(BF16)
    o_ref[0] = jnp.dot(a, w_ref[0].astype(BF16), preferred_element_type=F32) + b_ref[0]


def _mod_call(c8, w_mod, b_mod):
    depth, d, n3 = w_mod.shape
    tn = 512
    return pl.pallas_call(
        _mod_kernel,
        grid=(depth, n3 // tn),
        in_specs=[
            pl.BlockSpec((8, d), lambda l, j: (0, 0)),
            pl.BlockSpec((1, d, tn), lambda l, j: (l, 0, j)),
            pl.BlockSpec((1, 1, tn), lambda l, j: (l, 0, j)),
        ],
        out_specs=pl.BlockSpec((1, 8, tn), lambda l, j: (l, 0, j)),
        out_shape=jax.ShapeDtypeStruct((depth, 8, n3), F32),
        compiler_params=_cparams(("arbitrary", "arbitrary")),
        name="mod",
    )(c8, w_mod, b_mod.reshape(depth, 1, n3))


def _inproj_kernel(x_ref, scale_ref, shift_ref, g_ref, w_ref, o_ref, h_ref):
    @pl.when(pl.program_id(1) == 0)
    def _():
        x = x_ref[...]
        ms = jnp.mean(x * x, axis=-1, keepdims=True)
        gain = g_ref[...] * (1.0 + scale_ref[0])
        h_ref[...] = (x * lax.rsqrt(ms + RMS_EPS) * gain + shift_ref[0]).astype(BF16)

    o_ref[...] = jnp.dot(h_ref[...], w_ref[...], preferred_element_type=F32).astype(o_ref.dtype)


def _inproj_call(x2, scale, shift, g_pre, w_bf, layer, rows_per_seq, tm, tn, col_tiles=None):
    r, d = x2.shape
    n_col_tiles, col_tile_step = col_tiles if col_tiles else (w_bf.shape[2] // tn, 1)
    tiles_per_seq = rows_per_seq // tm
    return pl.pallas_call(
        _inproj_kernel,
        grid=(r // tm, n_col_tiles),
        in_specs=[
            pl.BlockSpec((tm, d), lambda i, j: (i, 0)),
            pl.BlockSpec((1, 1, d), lambda i, j: (i // tiles_per_seq, 0, 0)),
            pl.BlockSpec((1, 1, d), lambda i, j: (i // tiles_per_seq, 0, 0)),
            pl.BlockSpec((1, d), lambda i, j: (0, 0)),
            pl.BlockSpec((None, d, tn), lambda i, j: (layer, 0, j * col_tile_step)),
        ],
        out_specs=pl.BlockSpec((tm, tn), lambda i, j: (i, j)),
        out_shape=jax.ShapeDtypeStruct((r, n_col_tiles * tn), BF16),
        scratch_shapes=[pltpu.VMEM((tm, d), BF16)],
        compiler_params=_cparams(("arbitrary", "arbitrary")),
        name="inproj",
    )(x2, scale, shift, g_pre, w_bf)


def _head_mask(rows):
    lane = lax.broadcasted_iota(jnp.int32, (rows, V7X_LANES), 1)
    return lane < NA_HEAD_DIM


def _na_kernel(q_ref, k_ref, v_ref, kc_ref, vc_ref, t_ref, o_ref, bias_ref, *, grid_rows):
    @pl.when((pl.program_id(0) == 0) & (pl.program_id(1) == 0))
    def _():
        _na_expand_bias(t_ref, bias_ref)

    r0 = pl.program_id(1) * NA_GROUP_ROWS
    nq = NA_GROUP_ROWS * GRID_W
    first_half = _head_mask(nq)
    scale = NA_HEAD_DIM ** -0.5
    nt = (((1,), (1,)), ((), ()))
    gs = jnp.clip(r0 - NA_WIN_ROWS // 2, 0, grid_rows - NA_KEY_ROWS)
    cls = jnp.where(r0 == 0, 0, jnp.where(r0 == grid_rows - NA_GROUP_ROWS, 2, 1))
    k0 = pl.multiple_of(gs * GRID_W, GRID_W)
    for hp in range(NA_HEADS // 2):
        cs = slice(hp * V7X_LANES, (hp + 1) * V7X_LANES)
        qp = q_ref[:, cs] * jnp.asarray(scale, BF16)
        kp = k_ref[pl.ds(k0, NA_KEY_ROWS * GRID_W), cs]
        vp = v_ref[pl.ds(k0, NA_KEY_ROWS * GRID_W), cs]
        kcp = kc_ref[:, cs]
        vcp = vc_ref[:, cs]
        zero = jnp.zeros_like(qp)
        q2 = jnp.concatenate([jnp.where(first_half, qp, zero), jnp.where(first_half, zero, qp)], axis=0)
        s_b = lax.dot_general(q2, kp, nt, preferred_element_type=F32) + bias_ref[cls, hp]
        s_c = lax.dot_general(q2, kcp, nt, preferred_element_type=F32)
        m = jnp.maximum(jnp.max(s_b, axis=-1, keepdims=True), jnp.max(s_c, axis=-1, keepdims=True))
        p_b = jnp.exp(s_b - m)
        p_c = jnp.exp(s_c - m)
        denom = jnp.sum(p_b, axis=-1, keepdims=True) + jnp.sum(p_c, axis=-1, keepdims=True)
        o = (jnp.dot(p_b.astype(BF16), vp, preferred_element_type=F32)
             + jnp.dot(p_c.astype(BF16), vcp, preferred_element_type=F32)) / denom
        o_ref[:, cs] = jnp.where(first_half, o[0:nq], o[nq:]).astype(o_ref.dtype)


def _na_call(proj, projc, bias_tbls, layer, batch, seq, n_ctx):
    grid_rows = seq // GRID_W
    assert grid_rows % NA_GROUP_ROWS == 0 and grid_rows >= NA_KEY_ROWS
    tq = NA_GROUP_ROWS * GRID_W
    qb = seq // tq
    return pl.pallas_call(
        functools.partial(_na_kernel, grid_rows=grid_rows),
        grid=(batch, qb),
        in_specs=[
            pl.BlockSpec((tq, BRANCH), lambda b, r: (b * qb + r, COL_Q)),
            pl.BlockSpec((seq, BRANCH), lambda b, r: (b, COL_K)),
            pl.BlockSpec((seq, BRANCH), lambda b, r: (b, COL_V)),
            pl.BlockSpec((n_ctx, BRANCH), lambda b, r: (b, COL_K)),
            pl.BlockSpec((n_ctx, BRANCH), lambda b, r: (b, COL_V)),
            _layer_spec(bias_tbls.shape, layer),
        ],
        out_specs=pl.BlockSpec((tq, BRANCH), lambda b, r: (b * qb + r, 0)),
        out_shape=jax.ShapeDtypeStruct((batch * seq, BRANCH), BF16),
        scratch_shapes=[pltpu.VMEM((3, NA_HEADS // 2, 2 * tq, NA_KEY_ROWS * GRID_W), F32)],
        compiler_params=_cparams(("arbitrary", "arbitrary")),
        name="na_attn",
    )(proj, proj, proj, projc, projc, bias_tbls)


def _ctx_attn_kernel(q_ref, k_ref, v_ref, o_ref):
    n = q_ref.shape[0]
    first_half = _head_mask(n)
    scale = NA_HEAD_DIM ** -0.5
    nt = (((1,), (1,)), ((), ()))
    for hp in range(NA_HEADS // 2):
        cs = slice(hp * V7X_LANES, (hp + 1) * V7X_LANES)
        qp = q_ref[:, cs] * jnp.asarray(scale, BF16)
        kp = k_ref[:, cs]
        vp = v_ref[:, cs]
        zero = jnp.zeros_like(qp)
        q2 = jnp.concatenate([jnp.where(first_half, qp, zero), jnp.where(first_half, zero, qp)], axis=0)
        s = lax.dot_general(q2, kp, nt, preferred_element_type=F32)
        m = jnp.max(s, axis=-1, keepdims=True)
        p = jnp.exp(s - m)
        denom = jnp.sum(p, axis=-1, keepdims=True)
        o = jnp.dot(p.astype(BF16), vp, preferred_element_type=F32) / denom
        o_ref[:, cs] = jnp.where(first_half, o[0:n], o[n:]).astype(o_ref.dtype)


def _ctx_attn_call(projc, batch, n_ctx):
    return pl.pallas_call(
        _ctx_attn_kernel,
        grid=(batch,),
        in_specs=[
            pl.BlockSpec((n_ctx, BRANCH), lambda b: (b, COL_Q)),
            pl.BlockSpec((n_ctx, BRANCH), lambda b: (b, COL_K)),
            pl.BlockSpec((n_ctx, BRANCH), lambda b: (b, COL_V)),
        ],
        out_specs=pl.BlockSpec((n_ctx, BRANCH), lambda b: (b, 0)),
        out_shape=jax.ShapeDtypeStruct((batch * n_ctx, BRANCH), BF16),
        compiler_params=_cparams(("arbitrary",)),
        name="ctx_attn",
    )(projc, projc, projc)


def _na_bias_table(rpb):
    col = np.arange(GRID_W)
    col_start = np.clip(col - NA_WIN_COLS // 2, 0, GRID_W - NA_WIN_COLS)
    in_win = (col[None, :] >= col_start[:, None]) & (col[None, :] < col_start[:, None] + NA_WIN_COLS)
    dcol = np.clip(col[None, :] - col[:, None] + (NA_WIN_COLS - 1), 0, 2 * NA_WIN_COLS - 2)
    onehot = (dcol[..., None] == np.arange(2 * NA_WIN_COLS - 1)).astype(np.float32)
    t = jnp.einsum('hdc,qwc->hqdw', rpb.astype(F32), onehot, precision=lax.Precision.HIGHEST)
    t = jnp.where(in_win[None, :, None, :], t, NEG_INF)
    t = t.transpose(0, 2, 1, 3)
    return jnp.concatenate([t, t], axis=-1)


def _na_band_rows(cls, i):
    lo = (0, i, NA_KEY_ROWS - NA_WIN_ROWS)[cls]
    off = (NA_WIN_ROWS - 1 - i, NA_WIN_ROWS // 2 - 1 - i, NA_WIN_ROWS // 2 - 1 - lo - i)[cls]
    return lo, off


def _na_expand_bias(t_ref, bias_ref):
    lane = lax.broadcasted_iota(jnp.int32, (GRID_W, V7X_LANES), 1)
    left = lane < GRID_W
    neg = jnp.full((GRID_W, V7X_LANES), NEG_INF, F32)
    for cls in range(3):
        for hp in range(NA_HEADS // 2):
            for e in range(2):
                for i in range(NA_GROUP_ROWS):
                    lo, off = _na_band_rows(cls, i)
                    r0 = (e * NA_GROUP_ROWS + i) * GRID_W
                    for kt in range(NA_KEY_ROWS // 2):
                        halves = []
                        for kw in (2 * kt, 2 * kt + 1):
                            inside = lo <= kw < lo + NA_WIN_ROWS
                            halves.append(t_ref[2 * hp + e, kw + off] if inside else neg)
                        bias_ref[cls, hp, r0:r0 + GRID_W, kt * V7X_LANES:(kt + 1) * V7X_LANES] = (
                            jnp.where(left, halves[0], halves[1]))


def _cmul(ar, ai, br, bi):
    return ar * br - ai * bi, ar * bi + ai * br


def _ssm_weights(a_re, a_im, log_dt, b_re, b_im, c_re, c_im):
    t_len = SSM_CHUNK
    p, hg = SSM_STATE, SSM_GROUP_DIM
    lg = SSM_LANE_GROUPS
    kf, ef, mf, dec = [], [], [], []
    for d in range(2):
        ar, ai = a_re[d].astype(F32), a_im[d].astype(F32)
        dt = jnp.exp(log_dt[d].astype(F32))[:, None]
        mag = jnp.exp(ar * dt)
        abr, abi = mag * jnp.cos(ai * dt), mag * jnp.sin(ai * dt)
        den = ar * ar + ai * ai
        nr, ni = abr - 1.0, abi
        fr = (nr * ar + ni * ai) / den
        fi = (ni * ar - nr * ai) / den
        bbr, bbi = _cmul(fr[..., None], fi[..., None], b_re[d].astype(F32), b_im[d].astype(F32))
        pws = [(jnp.ones_like(abr), jnp.zeros_like(abr))]
        for _ in range(t_len):
            pws.append(_cmul(pws[-1][0], pws[-1][1], abr, abi))
        pr = jnp.stack([x[0] for x in pws])
        pi = jnp.stack([x[1] for x in pws])
        cr, ci = c_re[d].astype(F32), c_im[d].astype(F32)
        mr, mi = _cmul(cr[None], ci[None], pr[:t_len, :, None, :], pi[:t_len, :, None, :])
        kf.append(jnp.einsum('tghp,gpk->tghk', mr, bbr) - jnp.einsum('tghp,gpk->tghk', mi, bbi))
        e_pw = [pws[t_len - 1 - t] if d == 0 else pws[t] for t in range(t_len)]
        per, pei = jnp.stack([x[0] for x in e_pw]), jnp.stack([x[1] for x in e_pw])
        er, ei = _cmul(per[..., None], pei[..., None], bbr[None], bbi[None])
        ef.append((er, ei))
        c_pw = [pws[t + 1] if d == 0 else pws[t_len - t] for t in range(t_len)]
        pcr, pci = jnp.stack([x[0] for x in c_pw]), jnp.stack([x[1] for x in c_pw])
        m2r, m2i = _cmul(cr[None], ci[None], pcr[:, :, None, :], pci[:, :, None, :])
        mf.append((m2r, m2i))
        dec.append((pr[t_len], pi[t_len]))
    kc = [k.reshape(t_len, SSM_TILES, lg, hg, hg).transpose(0, 1, 4, 2, 3).reshape(
        t_len, SSM_TILES, hg, V7X_LANES) for k in kf]
    z = ([kc[1][t_len - 1 - i] for i in range(t_len - 1)] + [kc[0][0] + kc[1][0]]
         + [kc[0][i] for i in range(1, t_len)])
    ktoe = jnp.stack([jnp.stack(z[t_len - 1 - tp:2 * t_len - 1 - tp]) for tp in range(t_len)])
    parts = [ktoe.transpose(2, 0, 3, 1, 4).reshape(SSM_TILES, t_len, hg, t_len * V7X_LANES)]
    for d in range(2):
        for comp in ef[d]:
            parts.append(comp.reshape(t_len, SSM_TILES, lg, p, hg).transpose(1, 0, 4, 2, 3).reshape(
                SSM_TILES, t_len, hg, lg * p))
    c1 = jnp.concatenate(parts, axis=-1)
    rows = []
    for d in range(2):
        m2r, m2i = mf[d]
        for comp in (m2r, -m2i):
            rows.append(comp.reshape(t_len, SSM_TILES, lg, hg, p).transpose(1, 4, 0, 2, 3).reshape(
                SSM_TILES, p, t_len * V7X_LANES))
    cc = jnp.stack(rows, axis=1)
    decay = jnp.stack([x.reshape(SSM_TILES, SSM_TILES, V7X_LANES) for d in range(2) for x in dec[d]], axis=1)
    return c1.astype(BF16), cc.astype(BF16), decay


def _ssm_kernel(uc_ref, u_ref, c1_ref, cc_ref, dec_ref, dsk_ref, yc_ref, y_ref,
                w1_ref, wc_ref, uf_ref, lhs_ref, yi_ref, es_ref, *, n_ctx, seq, nb):
    t_len = SSM_CHUNK
    nc, lc = n_ctx // t_len, seq // t_len
    nj = nc + lc
    rows_b = n_ctx + seq
    kdim = t_len * V7X_LANES
    sw = SSM_LANE_GROUPS * SSM_STATE
    ntile = sw // V7X_LANES
    srows = nb * ntile
    hg = SSM_GROUP_DIM

    @pl.when(pl.program_id(1) == 0)
    def _():
        lane1 = lax.broadcasted_iota(jnp.int32, (1, kdim + 4 * sw), 1)
        grp1 = jnp.where(lane1 < kdim, (lane1 % V7X_LANES) // hg, ((lane1 - kdim) % sw) // SSM_STATE)
        lanec = lax.broadcasted_iota(jnp.int32, (1, kdim), 1)
        grpc = (lanec % V7X_LANES) // hg
        for t in range(t_len):
            row = c1_ref[0, t]
            for g in range(SSM_LANE_GROUPS):
                r0 = (t * SSM_LANE_GROUPS + g) * hg
                w1_ref[r0:r0 + hg, :] = jnp.where(grp1 == g, row, jnp.zeros_like(row))
        for part in range(4):
            blk = cc_ref[0, part]
            for g in range(SSM_LANE_GROUPS):
                r0 = part * sw + g * SSM_STATE
                wc_ref[r0:r0 + SSM_STATE, :] = jnp.where(grpc == g, blk, jnp.zeros_like(blk))

    for bl in range(nb):
        uf_ref[bl * rows_b:bl * rows_b + n_ctx, :] = uc_ref[bl * n_ctx:(bl + 1) * n_ctx, :].astype(F32)
        uf_ref[bl * rows_b + n_ctx:(bl + 1) * rows_b, :] = u_ref[bl * seq:(bl + 1) * seq, :].astype(F32)
    for t in range(t_len):
        lhs_ref[:, t * V7X_LANES:(t + 1) * V7X_LANES] = uf_ref[pl.ds(t, nb * nj, stride=t_len), :]
    lhs = lhs_ref[...].astype(BF16)
    yi_ref[...] = jnp.dot(lhs, w1_ref[:, 0:kdim], preferred_element_type=F32) + lhs_ref[...] * dsk_ref[0]
    for c in range(4):
        e = jnp.dot(lhs, w1_ref[:, kdim + c * sw:kdim + (c + 1) * sw], preferred_element_type=F32)
        for bl in range(nb):
            for k in range(ntile):
                es_ref[c, pl.ds(bl * ntile + k, nj, stride=srows), :] = (
                    e[bl * nj:(bl + 1) * nj, k * V7X_LANES:(k + 1) * V7X_LANES])

    afr, afi, abr, abi = dec_ref[0, 0], dec_ref[0, 1], dec_ref[0, 2], dec_ref[0, 3]

    def step(k, st):
        sfr, sfi, sbr, sbi = st
        jb = jnp.where(k < nc, nc - 1 - k, nj - 1 - (k - nc))
        rf = pl.ds(pl.multiple_of(k * srows, srows), srows)
        rb = pl.ds(pl.multiple_of(jb * srows, srows), srows)
        efr, efi = es_ref[0, rf, :], es_ref[1, rf, :]
        ebr, ebi = es_ref[2, rb, :], es_ref[3, rb, :]
        es_ref[0, rf, :] = sfr
        es_ref[1, rf, :] = sfi
        es_ref[2, rb, :] = sbr
        es_ref[3, rb, :] = sbi
        nfr, nfi = _cmul(afr, afi, sfr, sfi)
        nbr, nbi = _cmul(abr, abi, sbr, sbi)
        return nfr + efr, nfi + efi, nbr + ebr, nbi + ebi

    z = jnp.zeros((srows, V7X_LANES), F32)
    lax.fori_loop(0, nj, step, (z, z, z, z))

    carried = jnp.concatenate([
        jnp.concatenate([es_ref[c, pl.ds(bl * ntile + k, nj, stride=srows), :]
                         for c in range(4) for k in range(ntile)], axis=1)
        for bl in range(nb)], axis=0)
    yi_ref[...] += jnp.dot(carried.astype(BF16), wc_ref[...], preferred_element_type=F32)
    for t in range(t_len):
        uf_ref[pl.ds(t, nb * nj, stride=t_len), :] = yi_ref[:, t * V7X_LANES:(t + 1) * V7X_LANES]
    for bl in range(nb):
        yc_ref[bl * n_ctx:(bl + 1) * n_ctx, :] = uf_ref[bl * rows_b:bl * rows_b + n_ctx, :]
        y_ref[bl * seq:(bl + 1) * seq, :] = uf_ref[bl * rows_b + n_ctx:(bl + 1) * rows_b, :]


def _ssm_call(proj, projc, c1, cc, decay, dsk_tiled, layer, batch, seq, n_ctx, ctx_col):
    t_len = SSM_CHUNK
    nj = (n_ctx + seq) // t_len
    kdim = t_len * V7X_LANES
    sw4 = 4 * SSM_LANE_GROUPS * SSM_STATE
    ntile = sw4 // 4 // V7X_LANES
    col0 = COL_SSM_U * BRANCH // V7X_LANES
    colc = ctx_col // V7X_LANES
    nb = SSM_BATCH_PER_STEP if batch % SSM_BATCH_PER_STEP == 0 else 1
    assert decay.shape[-2] == nb * ntile
    return pl.pallas_call(
        functools.partial(_ssm_kernel, n_ctx=n_ctx, seq=seq, nb=nb),
        grid=(SSM_TILES, batch // nb),
        in_specs=[
            pl.BlockSpec((nb * n_ctx, V7X_LANES), lambda s, b: (b, colc + s)),
            pl.BlockSpec((nb * seq, V7X_LANES), lambda s, b: (b, col0 + s)),
            pl.BlockSpec((None, 1, t_len, SSM_GROUP_DIM, kdim + sw4), lambda s, b: (layer, s, 0, 0, 0)),
            pl.BlockSpec((None, 1, 4, SSM_STATE, kdim), lambda s, b: (layer, s, 0, 0, 0)),
            pl.BlockSpec((None, 1, 4, nb * ntile, V7X_LANES), lambda s, b: (layer, s, 0, 0, 0)),
            pl.BlockSpec((None, 1, 1, kdim), lambda s, b: (layer, s, 0, 0)),
        ],
        out_specs=[
            pl.BlockSpec((nb * n_ctx, V7X_LANES), lambda s, b: (b, s)),
            pl.BlockSpec((nb * seq, V7X_LANES), lambda s, b: (b, s)),
        ],
        out_shape=[
            jax.ShapeDtypeStruct((batch * n_ctx, BRANCH), F32),
            jax.ShapeDtypeStruct((batch * seq, BRANCH), F32),
        ],
        scratch_shapes=[
            pltpu.VMEM((kdim, kdim + sw4), BF16),
            pltpu.VMEM((sw4, kdim), BF16),
            pltpu.VMEM((nb * (n_ctx + seq), V7X_LANES), F32),
            pltpu.VMEM((nb * nj, kdim), F32),
            pltpu.VMEM((nb * nj, kdim), F32),
            pltpu.VMEM((4, nb * nj * ntile, V7X_LANES), F32),
        ],
        compiler_params=_cparams(("arbitrary", "arbitrary")),
        name="ssm",
    )(projc, proj, c1, cc, decay, dsk_tiled)


def _merge_kernel(x_ref, gate_ref, gpost_ref,
                  naz_ref, pu_ref, pz_ref, cx_ref, cb_ref, cc_ref, cz_ref, sz_ref,
                  lg0_ref, lg1_ref, lg2_ref, lg3_ref,
                  pu_prev_ref, pu_next_ref, cx_prev_ref, cx_next_ref, cc_prev_ref, cc_next_ref,
                  ona_ref, yssm_ref,
                  poolw_ref, pscale_ref, convw_ref, gluw_ref, bgate_ref, wbr_ref, wo_ref,
                  o_ref, pad_ref, s2_ref, s4_ref, *, tm, rows_per_seq):
    i = pl.program_id(0)
    tiles_per_seq = rows_per_seq // tm
    ti = i % tiles_per_seq
    has_prev = ti > 0
    has_next = ti < tiles_per_seq - 1
    t0 = ti * tm
    pos = t0 + lax.broadcasted_iota(jnp.int32, (tm, 1), 0)

    pad_ref[0:HALO, :] = jnp.where(has_prev, pu_prev_ref[...].astype(F32), 0.0)
    pad_ref[HALO:HALO + tm, :] = pu_ref[...].astype(F32)
    pad_ref[HALO + tm:2 * HALO + tm, :] = jnp.where(has_next, pu_next_ref[...].astype(F32), 0.0)
    base = HALO - max(POOL_WINDOWS) // 2
    n8, n4, n2 = tm + 16, tm + 24, tm + 32
    mixed = []
    for gi, w in enumerate(POOL_WINDOWS):
        cs = slice(gi * POOL_GROUP_DIM, (gi + 1) * POOL_GROUP_DIM)
        half = pad_ref
        if w >= 4:
            s2_ref[base:base + n2, cs] = pad_ref[base:base + n2, cs] + pad_ref[base + 1:base + 1 + n2, cs]
            half = s2_ref
        if w >= 8:
            s4_ref[base:base + n4, cs] = s2_ref[base:base + n4, cs] + s2_ref[base + 2:base + 2 + n4, cs]
            half = s4_ref
        if w >= 16:
            s2_ref[base:base + n8, cs] = s4_ref[base:base + n8, cs] + s4_ref[base + 4:base + 4 + n8, cs]
            half = s2_ref
        acc = half[HALO - w // 2:HALO - w // 2 + tm, cs] + half[HALO:HALO + tm, cs]
        lo = jnp.maximum(pos - w // 2, 0)
        hi = jnp.minimum(pos + w - w // 2, rows_per_seq)
        cnt = (hi - lo).astype(F32)
        pooled = acc / cnt - pad_ref[HALO:HALO + tm, cs]
        mixed.append(jnp.dot(pooled.astype(BF16), poolw_ref[gi], preferred_element_type=F32))
    o_pool = jnp.concatenate(mixed, axis=-1) * pscale_ref[...]

    pad_ref[0:HALO, :] = jnp.where(
        has_prev, cc_prev_ref[...].astype(F32) * cx_prev_ref[...].astype(F32), 0.0)
    pad_ref[HALO:HALO + tm, :] = cc_ref[...].astype(F32) * cx_ref[...].astype(F32)
    pad_ref[HALO + tm:2 * HALO + tm, :] = jnp.where(
        has_next, cc_next_ref[...].astype(F32) * cx_next_ref[...].astype(F32), 0.0)
    cw = convw_ref[...]
    conv = (pad_ref[HALO - 1:HALO - 1 + tm, :] * cw[0:1] + pad_ref[HALO:HALO + tm, :] * cw[1:2]
            + pad_ref[HALO + 1:HALO + 1 + tm, :] * cw[2:3])
    o_conv = cb_ref[...].astype(F32) * conv

    g = jax.nn.gelu(yssm_ref[...]).astype(BF16)
    gg = jnp.dot(g, gluw_ref[...], preferred_element_type=F32)
    ga = 0.5 * gg[:, 0:BRANCH]
    o_ssm = ga + ga * jnp.tanh(0.5 * gg[:, BRANCH:2 * BRANCH])

    outs = (ona_ref[...].astype(F32), o_pool, o_conv, o_ssm)
    zs = (naz_ref, pz_ref, cz_ref, sz_ref)
    lgs = (lg0_ref, lg1_ref, lg2_ref, lg3_ref)
    merged = None
    for bi in range(N_BRANCHES):
        hz = 0.5 * zs[bi][...].astype(F32)
        a = (outs[bi] * (hz + hz * jnp.tanh(hz))).astype(BF16)
        br = jnp.dot(a, wbr_ref[bi * BRANCH:(bi + 1) * BRANCH, :], preferred_element_type=F32)
        th = jnp.tanh(lgs[bi][...].astype(F32) + bgate_ref[:, bi * D_MODEL:(bi + 1) * D_MODEL])
        term = (1.0 + th) * br
        merged = term if merged is None else merged + term
    y = jnp.dot(merged.astype(BF16), wo_ref[...], preferred_element_type=F32)
    ms = jnp.mean(y * y, axis=-1, keepdims=True)
    gain = gpost_ref[...] * gate_ref[0]
    o_ref[...] = x_ref[...] + y * lax.rsqrt(ms + RMS_EPS) * gain


def _merge_call(x2, gate, g_post, proj, o_na, y_ssm, pool_w, pool_scale, conv_w, glu_w, b_gate,
                w_br, w_o, layer, rows_per_seq, tm):
    r, d = x2.shape
    tiles_per_seq = rows_per_seq // tm
    hb = tm // HALO
    n_halo_blocks = r // HALO

    def col(c):
        return pl.BlockSpec((tm, BRANCH), lambda i: (i, c))

    def lg(c):
        return pl.BlockSpec((tm, D_MODEL), lambda i: (i, COL_MERGE * BRANCH // D_MODEL + c))

    def prev(c):
        return pl.BlockSpec((HALO, BRANCH), lambda i: (jnp.maximum(i * hb - 1, 0), c))

    def nxt(c):
        return pl.BlockSpec((HALO, BRANCH), lambda i: (jnp.minimum((i + 1) * hb, n_halo_blocks - 1), c))

    in_specs = [
        pl.BlockSpec((tm, d), lambda i: (i, 0)),
        pl.BlockSpec((1, 1, d), lambda i: (i // tiles_per_seq, 0, 0)),
        pl.BlockSpec((1, d), lambda i: (0, 0)),
        col(COL_NA_Z), col(COL_POOL_U), col(COL_POOL_Z), col(COL_CONV_X), col(COL_CONV_B),
        col(COL_CONV_C), col(COL_CONV_Z), col(COL_SSM_Z),
        lg(0), lg(1), lg(2), lg(3),
        prev(COL_POOL_U), nxt(COL_POOL_U), prev(COL_CONV_X), nxt(COL_CONV_X),
        prev(COL_CONV_C), nxt(COL_CONV_C),
        pl.BlockSpec((tm, BRANCH), lambda i: (i, 0)),
        pl.BlockSpec((tm, BRANCH), lambda i: (i, 0)),
        _layer_spec(pool_w.shape, layer), _const_spec(pool_scale.shape), _const_spec(conv_w.shape),
        _layer_spec(glu_w.shape, layer), _const_spec(b_gate.shape), _layer_spec(w_br.shape, layer),
        _layer_spec(w_o.shape, layer),
    ]
    args = [x2, gate, g_post] + [proj] * 8 + [proj] * 4 + [proj] * 6 + [
        o_na, y_ssm, pool_w, pool_scale, conv_w, glu_w, b_gate, w_br, w_o]
    return pl.pallas_call(
        functools.partial(_merge_kernel, tm=tm, rows_per_seq=rows_per_seq),
        grid=(r // tm,),
        in_specs=in_specs,
        out_specs=pl.BlockSpec((tm, d), lambda i: (i, 0)),
        out_shape=jax.ShapeDtypeStruct((r, d), F32),
        scratch_shapes=[pltpu.VMEM((tm + 2 * HALO, BRANCH), F32)] * 3,
        compiler_params=_cparams(("arbitrary",)),
        name="merge",
    )(*args)


def _tile_rows(rows, pref):
    t = min(rows, pref)
    assert rows % t == 0
    return t


def kernel(x, c, ctx, c_ctx, w_mod, b_mod, g_pre, g_post, w_in, b_gate, na_rpb, pool_w, pool_scale,
           conv_w, ssm_a_re, ssm_a_im, ssm_log_dt, ssm_b_re, ssm_b_im, ssm_c_re, ssm_c_im, ssm_d,
           glu_w, w_br, w_o):
    batch, seq, d = x.shape
    n_ctx = ctx.shape[1]
    depth = w_mod.shape[0]
    assert d == D_MODEL and seq % (GRID_W * NA_WIN_ROWS) == 0 and batch + 1 <= 8
    assert n_ctx % HALO == 0 and w_in.shape[-1] == IN_TOTAL

    c8 = jnp.concatenate([c, c_ctx[None], jnp.zeros((7 - batch, d), F32)], axis=0)
    mod = _mod_call(c8, w_mod, b_mod)

    x2 = x.reshape(batch * seq, d)
    xc2 = ctx.reshape(batch * n_ctx, d)
    tm_in = _tile_rows(seq, INPROJ_ROW_TILE)
    tm_in_c = _tile_rows(batch * n_ctx, INPROJ_ROW_TILE)
    tm_mg = _tile_rows(seq, MERGE_ROW_TILE)
    tm_mg_c = _tile_rows(n_ctx, MERGE_ROW_TILE)
    tn_in = INPROJ_COL_TILE

    bias_tbls = jax.vmap(_na_bias_table)(na_rpb)
    c1s, ccs, decays = jax.vmap(_ssm_weights)(ssm_a_re, ssm_a_im, ssm_log_dt, ssm_b_re, ssm_b_im,
                                              ssm_c_re, ssm_c_im)
    if batch % SSM_BATCH_PER_STEP == 0:
        decays = jnp.tile(decays, (1, 1, 1, SSM_BATCH_PER_STEP, 1))
    dsks = jnp.tile(ssm_d.astype(F32).reshape(depth, SSM_TILES, 1, V7X_LANES), (1, 1, 1, SSM_CHUNK))
    col_scale = jnp.where(jnp.arange(IN_TOTAL) < COL_MERGE * BRANCH, 1.0, 0.5).astype(F32)
    w_in_bf = (w_in * col_scale).astype(BF16)
    pool_w_bf, glu_w_bf = pool_w.astype(BF16), glu_w.astype(BF16)
    w_br_bf, w_o_bf = w_br.astype(BF16), (0.5 * w_o).astype(BF16)

    for i in range(depth):
        with_ctx_out = i < depth - 1
        shift, scale, gate = (mod[i, :, k * d:(k + 1) * d] for k in range(3))
        lat = lambda v: v[:batch].reshape(batch, 1, d)
        cx = lambda v: v[batch:batch + 1].reshape(1, 1, d)
        gp = g_pre[i].reshape(1, d)
        gq = g_post[i].reshape(1, d)

        proj = _inproj_call(x2, lat(scale), lat(shift), gp, w_in_bf, i, seq, tm_in, tn_in)
        if with_ctx_out:
            projc = _inproj_call(xc2, cx(scale), cx(shift), gp, w_in_bf, i, batch * n_ctx, tm_in_c, tn_in)
            ctx_u_col = COL_SSM_U * BRANCH
        else:
            assert COL_V * BRANCH < tn_in and (COL_SSM_U * BRANCH) // tn_in == 2
            projc = _inproj_call(xc2, cx(scale), cx(shift), gp, w_in_bf, i, batch * n_ctx, tm_in_c, tn_in,
                                 col_tiles=(2, 2))
            ctx_u_col = COL_SSM_U * BRANCH - tn_in

        o_na = _na_call(proj, projc, bias_tbls, i, batch, seq, n_ctx)
        yc_ssm, y_ssm = _ssm_call(proj, projc, c1s, ccs, decays, dsks, i, batch, seq, n_ctx, ctx_u_col)

        mw = (pool_w_bf, pool_scale[i].reshape(1, BRANCH), conv_w[i], glu_w_bf,
              0.5 * b_gate[i].reshape(1, N_BRANCHES * d), w_br_bf, w_o_bf, i)
        x2 = _merge_call(x2, lat(gate), gq, proj, o_na, y_ssm, *mw, seq, tm_mg)
        if with_ctx_out:
            oc_na = _ctx_attn_call(projc, batch, n_ctx)
            gate_c = jnp.broadcast_to(cx(gate), (batch, 1, d))
            xc2 = _merge_call(xc2, gate_c, gq, projc, oc_na, yc_ssm, *mw, n_ctx, tm_mg_c)
    return x2.reshape(batch, seq, d)
```

```python
import functools

import jax
import jax.numpy as jnp
import numpy as np
from jax import lax
from jax.experimental import pallas as pl
from jax.experimental.pallas import tpu as pltpu

F32 = jnp.float32
BF16 = jnp.bfloat16

D_MODEL = 2048
GRID_W = 64
RMS_EPS = 1e-6
NEG_INF = -1e30
BRANCH = 512
N_BRANCHES = 4
NA_HEADS = 8
NA_HEAD_DIM = 64
NA_WIN_ROWS = 8
NA_WIN_COLS = 16
NA_GROUP_ROWS = 4
NA_KEY_ROWS = 12
POOL_WINDOWS = (2, 4, 8, 16)
POOL_GROUP_DIM = 128
SSM_GROUPS = 32
SSM_GROUP_DIM = 16
SSM_STATE = 64
IN_TOTAL = 12 * BRANCH + N_BRANCHES * D_MODEL
COL_Q, COL_K, COL_V, COL_NA_Z = 0, 1, 2, 3
COL_POOL_U, COL_POOL_Z = 4, 5
COL_CONV_X, COL_CONV_B, COL_CONV_C, COL_CONV_Z = 6, 7, 8, 9
COL_SSM_U, COL_SSM_Z = 10, 11
COL_MERGE = 12

V7X_LANES = 128
SSM_CHUNK = 8
SSM_LANE_GROUPS = V7X_LANES // SSM_GROUP_DIM
SSM_TILES = BRANCH // V7X_LANES
SSM_BATCH_PER_STEP = 2
HALO = 32
assert POOL_WINDOWS == (2, 4, 8, 16) and HALO >= 29 and HALO % 16 == 0
VMEM_LIMIT = 56 * 1024 * 1024
INPROJ_ROW_TILE = 1024
INPROJ_COL_TILE = 2048
MERGE_ROW_TILE = 256


def _cparams(sem):
    return pltpu.CompilerParams(dimension_semantics=sem, vmem_limit_bytes=VMEM_LIMIT)


def _const_spec(shape):
    nd = len(shape)
    return pl.BlockSpec(shape, lambda *_: (0,) * nd, pipeline_mode=pl.Buffered(1))


def _layer_spec(stacked_shape, layer):
    nd = len(stacked_shape) - 1
    return pl.BlockSpec((None,) + tuple(stacked_shape[1:]), lambda *_: (layer,) + (0,) * nd,
                        pipeline_mode=pl.Buffered(1))


def _mod_kernel(c_ref, w_ref, b_ref, o_ref):
    c = c_ref[...]
    a = (c * jax.nn.sigmoid(c)).astype(BF16)
    o_ref[0] = jnp.dot(a, w_ref[0].astype(BF16), preferred_element_type=F32) + b_ref[0]


def _mod_call(c8, w_mod, b_mod):
    depth, d, n3 = w_mod.shape
    tn = 512
    return pl.pallas_call(
        _mod_kernel,
        grid=(depth, n3 // tn),
        in_specs=[
            pl.BlockSpec((8, d), lambda l, j: (0, 0)),
            pl.BlockSpec((1, d, tn), lambda l, j: (l, 0, j)),
            pl.BlockSpec((1, 1, tn), lambda l, j: (l, 0, j)),
        ],
        out_specs=pl.BlockSpec((1, 8, tn), lambda l, j: (l, 0, j)),
        out_shape=jax.ShapeDtypeStruct((depth, 8, n3), F32),
        compiler_params=_cparams(("arbitrary", "arbitrary")),
        name="mod",
    )(c8, w_mod, b_mod.reshape(depth, 1, n3))


def _inproj_kernel(x_ref, scale_ref, shift_ref, g_ref, w_ref, o_ref, h_ref):
    @pl.when(pl.program_id(1) == 0)
    def _():
        x = x_ref[...]
        ms = jnp.mean(x * x, axis=-1, keepdims=True)
        gain = g_ref[...] * (1.0 + scale_ref[0])
        h_ref[...] = (x * lax.rsqrt(ms + RMS_EPS) * gain + shift_ref[0]).astype(BF16)

    o_ref[...] = jnp.dot(h_ref[...], w_ref[...], preferred_element_type=F32).astype(o_ref.dtype)


def _inproj_call(x2, scale, shift, g_pre, w_bf, layer, rows_per_seq, tm, tn, col_tiles=None):
    r, d = x2.shape
    n_col_tiles, col_tile_step = col_tiles if col_tiles else (w_bf.shape[2] // tn, 1)
    tiles_per_seq = rows_per_seq // tm
    return pl.pallas_call(
        _inproj_kernel,
        grid=(r // tm, n_col_tiles),
        in_specs=[
            pl.BlockSpec((tm, d), lambda i, j: (i, 0)),
            pl.BlockSpec((1, 1, d), lambda i, j: (i // tiles_per_seq, 0, 0)),
            pl.BlockSpec((1, 1, d), lambda i, j: (i // tiles_per_seq, 0, 0)),
            pl.BlockSpec((1, d), lambda i, j: (0, 0)),
            pl.BlockSpec((None, d, tn), lambda i, j: (layer, 0, j * col_tile_step)),
        ],
        out_specs=pl.BlockSpec((tm, tn), lambda i, j: (i, j)),
        out_shape=jax.ShapeDtypeStruct((r, n_col_tiles * tn), BF16),
        scratch_shapes=[pltpu.VMEM((tm, d), BF16)],
        compiler_params=_cparams(("arbitrary", "arbitrary")),
        name="inproj",
    )(x2, scale, shift, g_pre, w_bf)


def _head_mask(rows):
    lane = lax.broadcasted_iota(jnp.int32, (rows, V7X_LANES), 1)
    return lane < NA_HEAD_DIM


def _na_kernel(q_ref, k_ref, v_ref, kc_ref, vc_ref, t_ref, o_ref, bias_ref, *, grid_rows):
    @pl.when((pl.program_id(0) == 0) & (pl.program_id(1) == 0))
    def _():
        _na_expand_bias(t_ref, bias_ref)

    r0 = pl.program_id(1) * NA_GROUP_ROWS
    nq = NA_GROUP_ROWS * GRID_W
    first_half = _head_mask(nq)
    scale = NA_HEAD_DIM ** -0.5
    nt = (((1,), (1,)), ((), ()))
    gs = jnp.clip(r0 - NA_WIN_ROWS // 2, 0, grid_rows - NA_KEY_ROWS)
    cls = jnp.where(r0 == 0, 0, jnp.where(r0 == grid_rows - NA_GROUP_ROWS, 2, 1))
    k0 = pl.multiple_of(gs * GRID_W, GRID_W)
    def scores(hp):
        cs = slice(hp * V7X_LANES, (hp + 1) * V7X_LANES)
        qp = q_ref[:, cs] * jnp.asarray(scale, BF16)
        kp = k_ref[pl.ds(k0, NA_KEY_ROWS * GRID_W), cs]
        kcp = kc_ref[:, cs]
        zero = jnp.zeros_like(qp)
        q2 = jnp.concatenate([jnp.where(first_half, qp, zero), jnp.where(first_half, zero, qp)], axis=0)
        s_b = lax.dot_general(q2, kp, nt, preferred_element_type=F32) + bias_ref[cls, hp]
        s_c = lax.dot_general(q2, kcp, nt, preferred_element_type=F32)
        return s_b, s_c

    def attend(hp, s_b, s_c):
        cs = slice(hp * V7X_LANES, (hp + 1) * V7X_LANES)
        vp = v_ref[pl.ds(k0, NA_KEY_ROWS * GRID_W), cs]
        vcp = vc_ref[:, cs]
        m = jnp.maximum(jnp.max(s_b, axis=-1, keepdims=True), jnp.max(s_c, axis=-1, keepdims=True))
        p_b = jnp.exp(s_b - m)
        p_c = jnp.exp(s_c - m)
        denom = jnp.sum(p_b, axis=-1, keepdims=True) + jnp.sum(p_c, axis=-1, keepdims=True)
        o = (jnp.dot(p_b.astype(BF16), vp, preferred_element_type=F32)
             + jnp.dot(p_c.astype(BF16), vcp, preferred_element_type=F32)) / denom
        o_ref[:, cs] = jnp.where(first_half, o[0:nq], o[nq:]).astype(o_ref.dtype)

    n_pairs = NA_HEADS // 2
    sc = scores(0)
    for hp in range(n_pairs):
        nxt = scores(hp + 1) if hp + 1 < n_pairs else None
        attend(hp, *sc)
        sc = nxt


def _na_call(proj, projc, bias_tbls, layer, batch, seq, n_ctx):
    grid_rows = seq // GRID_W
    assert grid_rows % NA_GROUP_ROWS == 0 and grid_rows >= NA_KEY_ROWS
    tq = NA_GROUP_ROWS * GRID_W
    qb = seq // tq
    return pl.pallas_call(
        functools.partial(_na_kernel, grid_rows=grid_rows),
        grid=(batch, qb),
        in_specs=[
            pl.BlockSpec((tq, BRANCH), lambda b, r: (b * qb + r, COL_Q)),
            pl.BlockSpec((seq, BRANCH), lambda b, r: (b, COL_K)),
            pl.BlockSpec((seq, BRANCH), lambda b, r: (b, COL_V)),
            pl.BlockSpec((n_ctx, BRANCH), lambda b, r: (b, COL_K)),
            pl.BlockSpec((n_ctx, BRANCH), lambda b, r: (b, COL_V)),
            _layer_spec(bias_tbls.shape, layer),
        ],
        out_specs=pl.BlockSpec((tq, BRANCH), lambda b, r: (b * qb + r, 0)),
        out_shape=jax.ShapeDtypeStruct((batch * seq, BRANCH), BF16),
        scratch_shapes=[pltpu.VMEM((3, NA_HEADS // 2, 2 * tq, NA_KEY_ROWS * GRID_W), F32)],
        compiler_params=_cparams(("arbitrary", "arbitrary")),
        name="na_attn",
    )(proj, proj, proj, projc, projc, bias_tbls)


def _ctx_attn_kernel(q_ref, k_ref, v_ref, o_ref):
    n = q_ref.shape[0]
    first_half = _head_mask(n)
    scale = NA_HEAD_DIM ** -0.5
    nt = (((1,), (1,)), ((), ()))
    for hp in range(NA_HEADS // 2):
        cs = slice(hp * V7X_LANES, (hp + 1) * V7X_LANES)
        qp = q_ref[:, cs] * jnp.asarray(scale, BF16)
        kp = k_ref[:, cs]
        vp = v_ref[:, cs]
        zero = jnp.zeros_like(qp)
        q2 = jnp.concatenate([jnp.where(first_half, qp, zero), jnp.where(first_half, zero, qp)], axis=0)
        s = lax.dot_general(q2, kp, nt, preferred_element_type=F32)
        m = jnp.max(s, axis=-1, keepdims=True)
        p = jnp.exp(s - m)
        denom = jnp.sum(p, axis=-1, keepdims=True)
        o = jnp.dot(p.astype(BF16), vp, preferred_element_type=F32) / denom
        o_ref[:, cs] = jnp.where(first_half, o[0:n], o[n:]).astype(o_ref.dtype)


def _ctx_attn_call(projc, batch, n_ctx):
    return pl.pallas_call(
        _ctx_attn_kernel,
        grid=(batch,),
        in_specs=[
            pl.BlockSpec((n_ctx, BRANCH), lambda b: (b, COL_Q)),
            pl.BlockSpec((n_ctx, BRANCH), lambda b: (b, COL_K)),
            pl.BlockSpec((n_ctx, BRANCH), lambda b: (b, COL_V)),
        ],
        out_specs=pl.BlockSpec((n_ctx, BRANCH), lambda b: (b, 0)),
        out_shape=jax.ShapeDtypeStruct((batch * n_ctx, BRANCH), BF16),
        compiler_params=_cparams(("arbitrary",)),
        name="ctx_attn",
    )(projc, projc, projc)


def _na_bias_table(rpb):
    col = np.arange(GRID_W)
    col_start = np.clip(col - NA_WIN_COLS // 2, 0, GRID_W - NA_WIN_COLS)
    in_win = (col[None, :] >= col_start[:, None]) & (col[None, :] < col_start[:, None] + NA_WIN_COLS)
    dcol = np.clip(col[None, :] - col[:, None] + (NA_WIN_COLS - 1), 0, 2 * NA_WIN_COLS - 2)
    onehot = (dcol[..., None] == np.arange(2 * NA_WIN_COLS - 1)).astype(np.float32)
    t = jnp.einsum('hdc,qwc->hqdw', rpb.astype(F32), onehot, precision=lax.Precision.HIGHEST)
    t = jnp.where(in_win[None, :, None, :], t, NEG_INF)
    t = t.transpose(0, 2, 1, 3)
    return jnp.concatenate([t, t], axis=-1)


def _na_band_rows(cls, i):
    lo = (0, i, NA_KEY_ROWS - NA_WIN_ROWS)[cls]
    off = (NA_WIN_ROWS - 1 - i, NA_WIN_ROWS // 2 - 1 - i, NA_WIN_ROWS // 2 - 1 - lo - i)[cls]
    return lo, off


def _na_expand_bias(t_ref, bias_ref):
    lane = lax.broadcasted_iota(jnp.int32, (GRID_W, V7X_LANES), 1)
    left = lane < GRID_W
    neg = jnp.full((GRID_W, V7X_LANES), NEG_INF, F32)
    for cls in range(3):
        for hp in range(NA_HEADS // 2):
            for e in range(2):
                for i in range(NA_GROUP_ROWS):
                    lo, off = _na_band_rows(cls, i)
                    r0 = (e * NA_GROUP_ROWS + i) * GRID_W
                    for kt in range(NA_KEY_ROWS // 2):
                        halves = []
                        for kw in (2 * kt, 2 * kt + 1):
                            inside = lo <= kw < lo + NA_WIN_ROWS
                            halves.append(t_ref[2 * hp + e, kw + off] if inside else neg)
                        bias_ref[cls, hp, r0:r0 + GRID_W, kt * V7X_LANES:(kt + 1) * V7X_LANES] = (
                            jnp.where(left, halves[0], halves[1]))


def _cmul(ar, ai, br, bi):
    return ar * br - ai * bi, ar * bi + ai * br


def _ssm_weights(a_re, a_im, log_dt, b_re, b_im, c_re, c_im):
    t_len = SSM_CHUNK
    p, hg = SSM_STATE, SSM_GROUP_DIM
    lg = SSM_LANE_GROUPS
    kf, ef, mf, dec = [], [], [], []
    for d in range(2):
        ar, ai = a_re[d].astype(F32), a_im[d].astype(F32)
        dt = jnp.exp(log_dt[d].astype(F32))[:, None]
        mag = jnp.exp(ar * dt)
        abr, abi = mag * jnp.cos(ai * dt), mag * jnp.sin(ai * dt)
        den = ar * ar + ai * ai
        nr, ni = abr - 1.0, abi
        fr = (nr * ar + ni * ai) / den
        fi = (ni * ar - nr * ai) / den
        bbr, bbi = _cmul(fr[..., None], fi[..., None], b_re[d].astype(F32), b_im[d].astype(F32))
        pws = [(jnp.ones_like(abr), jnp.zeros_like(abr))]
        for _ in range(t_len):
            pws.append(_cmul(pws[-1][0], pws[-1][1], abr, abi))
        pr = jnp.stack([x[0] for x in pws])
        pi = jnp.stack([x[1] for x in pws])
        cr, ci = c_re[d].astype(F32), c_im[d].astype(F32)
        mr, mi = _cmul(cr[None], ci[None], pr[:t_len, :, None, :], pi[:t_len, :, None, :])
        kf.append(jnp.einsum('tghp,gpk->tghk', mr, bbr) - jnp.einsum('tghp,gpk->tghk', mi, bbi))
        e_pw = [pws[t_len - 1 - t] if d == 0 else pws[t] for t in range(t_len)]
        per, pei = jnp.stack([x[0] for x in e_pw]), jnp.stack([x[1] for x in e_pw])
        er, ei = _cmul(per[..., None], pei[..., None], bbr[None], bbi[None])
        ef.append((er, ei))
        c_pw = [pws[t + 1] if d == 0 else pws[t_len - t] for t in range(t_len)]
        pcr, pci = jnp.stack([x[0] for x in c_pw]), jnp.stack([x[1] for x in c_pw])
        m2r, m2i = _cmul(cr[None], ci[None], pcr[:, :, None, :], pci[:, :, None, :])
        mf.append((m2r, m2i))
        dec.append((pr[t_len], pi[t_len]))
    kc = [k.reshape(t_len, SSM_TILES, lg, hg, hg).transpose(0, 1, 4, 2, 3).reshape(
        t_len, SSM_TILES, hg, V7X_LANES) for k in kf]
    z = ([kc[1][t_len - 1 - i] for i in range(t_len - 1)] + [kc[0][0] + kc[1][0]]
         + [kc[0][i] for i in range(1, t_len)])
    ktoe = jnp.stack([jnp.stack(z[t_len - 1 - tp:2 * t_len - 1 - tp]) for tp in range(t_len)])
    parts = [ktoe.transpose(2, 0, 3, 1, 4).reshape(SSM_TILES, t_len, hg, t_len * V7X_LANES)]
    for d in range(2):
        for comp in ef[d]:
            parts.append(comp.reshape(t_len, SSM_TILES, lg, p, hg).transpose(1, 0, 4, 2, 3).reshape(
                SSM_TILES, t_len, hg, lg * p))
    c1 = jnp.concatenate(parts, axis=-1)
    rows = []
    for d in range(2):
        m2r, m2i = mf[d]
        for comp in (m2r, -m2i):
            rows.append(comp.reshape(t_len, SSM_TILES, lg, hg, p).transpose(1, 4, 0, 2, 3).reshape(
                SSM_TILES, p, t_len * V7X_LANES))
    cc = jnp.stack(rows, axis=1)
    decay = jnp.stack([x.reshape(SSM_TILES, SSM_TILES, V7X_LANES) for d in range(2) for x in dec[d]], axis=1)
    return c1.astype(BF16), cc.astype(BF16), decay


def _ssm_kernel(uc_ref, u_ref, c1_ref, cc_ref, dec_ref, dsk_ref, yc_ref, y_ref,
                w1_ref, wc_ref, uf_ref, lhs_ref, yi_ref, es_ref, *, n_ctx, seq, nb):
    t_len = SSM_CHUNK
    nc, lc = n_ctx // t_len, seq // t_len
    nj = nc + lc
    rows_b = n_ctx + seq
    kdim = t_len * V7X_LANES
    sw = SSM_LANE_GROUPS * SSM_STATE
    ntile = sw // V7X_LANES
    srows = nb * ntile
    hg = SSM_GROUP_DIM

    @pl.when(pl.program_id(1) == 0)
    def _():
        lane1 = lax.broadcasted_iota(jnp.int32, (1, kdim + 4 * sw), 1)
        grp1 = jnp.where(lane1 < kdim, (lane1 % V7X_LANES) // hg, ((lane1 - kdim) % sw) // SSM_STATE)
        lanec = lax.broadcasted_iota(jnp.int32, (1, kdim), 1)
        grpc = (lanec % V7X_LANES) // hg
        for t in range(t_len):
            row = c1_ref[0, t]
            for g in range(SSM_LANE_GROUPS):
                r0 = (t * SSM_LANE_GROUPS + g) * hg
                w1_ref[r0:r0 + hg, :] = jnp.where(grp1 == g, row, jnp.zeros_like(row))
        for part in range(4):
            blk = cc_ref[0, part]
            for g in range(SSM_LANE_GROUPS):
                r0 = part * sw + g * SSM_STATE
                wc_ref[r0:r0 + SSM_STATE, :] = jnp.where(grpc == g, blk, jnp.zeros_like(blk))

    for bl in range(nb):
        uf_ref[bl * rows_b:bl * rows_b + n_ctx, :] = uc_ref[bl * n_ctx:(bl + 1) * n_ctx, :].astype(F32)
        uf_ref[bl * rows_b + n_ctx:(bl + 1) * rows_b, :] = u_ref[bl * seq:(bl + 1) * seq, :].astype(F32)
    for t in range(t_len):
        lhs_ref[:, t * V7X_LANES:(t + 1) * V7X_LANES] = uf_ref[pl.ds(t, nb * nj, stride=t_len), :]
    lhs = lhs_ref[...].astype(BF16)
    yi_ref[...] = jnp.dot(lhs, w1_ref[:, 0:kdim], preferred_element_type=F32) + lhs_ref[...] * dsk_ref[0]
    for c in range(4):
        e = jnp.dot(lhs, w1_ref[:, kdim + c * sw:kdim + (c + 1) * sw], preferred_element_type=F32)
        for bl in range(nb):
            for k in range(ntile):
                es_ref[c, pl.ds(bl * ntile + k, nj, stride=srows), :] = (
                    e[bl * nj:(bl + 1) * nj, k * V7X_LANES:(k + 1) * V7X_LANES])

    afr, afi, abr, abi = dec_ref[0, 0], dec_ref[0, 1], dec_ref[0, 2], dec_ref[0, 3]

    def step(k, st):
        sfr, sfi, sbr, sbi = st
        jb = jnp.where(k < nc, nc - 1 - k, nj - 1 - (k - nc))
        rf = pl.ds(pl.multiple_of(k * srows, srows), srows)
        rb = pl.ds(pl.multiple_of(jb * srows, srows), srows)
        efr, efi = es_ref[0, rf, :], es_ref[1, rf, :]
        ebr, ebi = es_ref[2, rb, :], es_ref[3, rb, :]
        es_ref[0, rf, :] = sfr
        es_ref[1, rf, :] = sfi
        es_ref[2, rb, :] = sbr
        es_ref[3, rb, :] = sbi
        nfr, nfi = _cmul(afr, afi, sfr, sfi)
        nbr, nbi = _cmul(abr, abi, sbr, sbi)
        return nfr + efr, nfi + efi, nbr + ebr, nbi + ebi

    z = jnp.zeros((srows, V7X_LANES), F32)
    lax.fori_loop(0, nj, step, (z, z, z, z))

    carried = jnp.concatenate([
        jnp.concatenate([es_ref[c, pl.ds(bl * ntile + k, nj, stride=srows), :]
                         for c in range(4) for k in range(ntile)], axis=1)
        for bl in range(nb)], axis=0)
    yi_ref[...] += jnp.dot(carried.astype(BF16), wc_ref[...], preferred_element_type=F32)
    for t in range(t_len):
        uf_ref[pl.ds(t, nb * nj, stride=t_len), :] = yi_ref[:, t * V7X_LANES:(t + 1) * V7X_LANES]
    for bl in range(nb):
        yc_ref[bl * n_ctx:(bl + 1) * n_ctx, :] = uf_ref[bl * rows_b:bl * rows_b + n_ctx, :]
        y_ref[bl * seq:(bl + 1) * seq, :] = uf_ref[bl * rows_b + n_ctx:(bl + 1) * rows_b, :]


def _ssm_call(proj, projc, c1, cc, decay, dsk_tiled, layer, batch, seq, n_ctx, ctx_col):
    t_len = SSM_CHUNK
    nj = (n_ctx + seq) // t_len
    kdim = t_len * V7X_LANES
    sw4 = 4 * SSM_LANE_GROUPS * SSM_STATE
    ntile = sw4 // 4 // V7X_LANES
    col0 = COL_SSM_U * BRANCH // V7X_LANES
    colc = ctx_col // V7X_LANES
    nb = SSM_BATCH_PER_STEP if batch % SSM_BATCH_PER_STEP == 0 else 1
    assert decay.shape[-2] == nb * ntile
    return pl.pallas_call(
        functools.partial(_ssm_kernel, n_ctx=n_ctx, seq=seq, nb=nb),
        grid=(SSM_TILES, batch // nb),
        in_specs=[
            pl.BlockSpec((nb * n_ctx, V7X_LANES), lambda s, b: (b, colc + s)),
            pl.BlockSpec((nb * seq, V7X_LANES), lambda s, b: (b, col0 + s)),
            pl.BlockSpec((None, 1, t_len, SSM_GROUP_DIM, kdim + sw4), lambda s, b: (layer, s, 0, 0, 0)),
            pl.BlockSpec((None, 1, 4, SSM_STATE, kdim), lambda s, b: (layer, s, 0, 0, 0)),
            pl.BlockSpec((None, 1, 4, nb * ntile, V7X_LANES), lambda s, b: (layer, s, 0, 0, 0)),
            pl.BlockSpec((None, 1, 1, kdim), lambda s, b: (layer, s, 0, 0)),
        ],
        out_specs=[
            pl.BlockSpec((nb * n_ctx, V7X_LANES), lambda s, b: (b, s)),
            pl.BlockSpec((nb * seq, V7X_LANES), lambda s, b: (b, s)),
        ],
        out_shape=[
            jax.ShapeDtypeStruct((batch * n_ctx, BRANCH), F32),
            jax.ShapeDtypeStruct((batch * seq, BRANCH), F32),
        ],
        scratch_shapes=[
            pltpu.VMEM((kdim, kdim + sw4), BF16),
            pltpu.VMEM((sw4, kdim), BF16),
            pltpu.VMEM((nb * (n_ctx + seq), V7X_LANES), F32),
            pltpu.VMEM((nb * nj, kdim), F32),
            pltpu.VMEM((nb * nj, kdim), F32),
            pltpu.VMEM((4, nb * nj * ntile, V7X_LANES), F32),
        ],
        compiler_params=_cparams(("arbitrary", "arbitrary")),
        name="ssm",
    )(projc, proj, c1, cc, decay, dsk_tiled)


def _merge_kernel(x_ref, gate_ref, gpost_ref,
                  naz_ref, pu_ref, pz_ref, cx_ref, cb_ref, cc_ref, cz_ref, sz_ref,
                  lg0_ref, lg1_ref, lg2_ref, lg3_ref,
                  pu_prev_ref, pu_next_ref, cx_prev_ref, cx_next_ref, cc_prev_ref, cc_next_ref,
                  ona_ref, yssm_ref,
                  poolw_ref, pscale_ref, convw_ref, gluw_ref, bgate_ref, wbr_ref, wo_ref,
                  o_ref, pad_ref, s2_ref, s4_ref, *, tm, rows_per_seq):
    i = pl.program_id(0)
    tiles_per_seq = rows_per_seq // tm
    ti = i % tiles_per_seq
    has_prev = ti > 0
    has_next = ti < tiles_per_seq - 1
    t0 = ti * tm
    pos = t0 + lax.broadcasted_iota(jnp.int32, (tm, 1), 0)
    zs = (naz_ref, pz_ref, cz_ref, sz_ref)
    lgs = (lg0_ref, lg1_ref, lg2_ref, lg3_ref)

    def gated_projection(bi, out):
        hz = 0.5 * zs[bi][...].astype(F32)
        a = (out * (hz + hz * jnp.tanh(hz))).astype(BF16)
        br = jnp.dot(a, wbr_ref[bi * BRANCH:(bi + 1) * BRANCH, :], preferred_element_type=F32)
        th = jnp.tanh(lgs[bi][...].astype(F32) + bgate_ref[:, bi * D_MODEL:(bi + 1) * D_MODEL])
        return (1.0 + th) * br

    merged = gated_projection(0, ona_ref[...].astype(F32))

    pad_ref[0:HALO, :] = jnp.where(
        has_prev, cc_prev_ref[...].astype(F32) * cx_prev_ref[...].astype(F32), 0.0)
    pad_ref[HALO:HALO + tm, :] = cc_ref[...].astype(F32) * cx_ref[...].astype(F32)
    pad_ref[HALO + tm:2 * HALO + tm, :] = jnp.where(
        has_next, cc_next_ref[...].astype(F32) * cx_next_ref[...].astype(F32), 0.0)
    cw = convw_ref[...]
    conv = (pad_ref[HALO - 1:HALO - 1 + tm, :] * cw[0:1] + pad_ref[HALO:HALO + tm, :] * cw[1:2]
            + pad_ref[HALO + 1:HALO + 1 + tm, :] * cw[2:3])
    merged = merged + gated_projection(2, cb_ref[...].astype(F32) * conv)

    g = jax.nn.gelu(yssm_ref[...]).astype(BF16)
    gg = jnp.dot(g, gluw_ref[...], preferred_element_type=F32)
    ga = 0.5 * gg[:, 0:BRANCH]
    o_ssm = ga + ga * jnp.tanh(0.5 * gg[:, BRANCH:2 * BRANCH])
    merged = merged + gated_projection(3, o_ssm)

    pad_ref[0:HALO, :] = jnp.where(has_prev, pu_prev_ref[...].astype(F32), 0.0)
    pad_ref[HALO:HALO + tm, :] = pu_ref[...].astype(F32)
    pad_ref[HALO + tm:2 * HALO + tm, :] = jnp.where(has_next, pu_next_ref[...].astype(F32), 0.0)
    base = HALO - max(POOL_WINDOWS) // 2
    n8, n4, n2 = tm + 16, tm + 24, tm + 32
    mixed = []
    for gi, w in enumerate(POOL_WINDOWS):
        cs = slice(gi * POOL_GROUP_DIM, (gi + 1) * POOL_GROUP_DIM)
        half = pad_ref
        if w >= 4:
            s2_ref[base:base + n2, cs] = pad_ref[base:base + n2, cs] + pad_ref[base + 1:base + 1 + n2, cs]
            half = s2_ref
        if w >= 8:
            s4_ref[base:base + n4, cs] = s2_ref[base:base + n4, cs] + s2_ref[base + 2:base + 2 + n4, cs]
            half = s4_ref
        if w >= 16:
            s2_ref[base:base + n8, cs] = s4_ref[base:base + n8, cs] + s4_ref[base + 4:base + 4 + n8, cs]
            half = s2_ref
        acc = half[HALO - w // 2:HALO - w // 2 + tm, cs] + half[HALO:HALO + tm, cs]
        lo = jnp.maximum(pos - w // 2, 0)
        hi = jnp.minimum(pos + w - w // 2, rows_per_seq)
        cnt = (hi - lo).astype(F32)
        pooled = acc / cnt - pad_ref[HALO:HALO + tm, cs]
        mixed.append(jnp.dot(pooled.astype(BF16), poolw_ref[gi], preferred_element_type=F32))
    merged = merged + gated_projection(1, jnp.concatenate(mixed, axis=-1) * pscale_ref[...])

    y = jnp.dot(merged.astype(BF16), wo_ref[...], preferred_element_type=F32)
    ms = jnp.mean(y * y, axis=-1, keepdims=True)
    gain = gpost_ref[...] * gate_ref[0]
    o_ref[...] = x_ref[...] + y * lax.rsqrt(ms + RMS_EPS) * gain


def _merge_call(x2, gate, g_post, proj, o_na, y_ssm, pool_w, pool_scale, conv_w, glu_w, b_gate,
                w_br, w_o, layer, rows_per_seq, tm):
    r, d = x2.shape
    tiles_per_seq = rows_per_seq // tm
    hb = tm // HALO
    n_halo_blocks = r // HALO

    def col(c):
        return pl.BlockSpec((tm, BRANCH), lambda i: (i, c))

    def lg(c):
        return pl.BlockSpec((tm, D_MODEL), lambda i: (i, COL_MERGE * BRANCH // D_MODEL + c))

    def prev(c):
        return pl.BlockSpec((HALO, BRANCH), lambda i: (jnp.maximum(i * hb - 1, 0), c))

    def nxt(c):
        return pl.BlockSpec((HALO, BRANCH), lambda i: (jnp.minimum((i + 1) * hb, n_halo_blocks - 1), c))

    in_specs = [
        pl.BlockSpec((tm, d), lambda i: (i, 0)),
        pl.BlockSpec((1, 1, d), lambda i: (i // tiles_per_seq, 0, 0)),
        pl.BlockSpec((1, d), lambda i: (0, 0)),
        col(COL_NA_Z), col(COL_POOL_U), col(COL_POOL_Z), col(COL_CONV_X), col(COL_CONV_B),
        col(COL_CONV_C), col(COL_CONV_Z), col(COL_SSM_Z),
        lg(0), lg(1), lg(2), lg(3),
        prev(COL_POOL_U), nxt(COL_POOL_U), prev(COL_CONV_X), nxt(COL_CONV_X),
        prev(COL_CONV_C), nxt(COL_CONV_C),
        pl.BlockSpec((tm, BRANCH), lambda i: (i, 0)),
        pl.BlockSpec((tm, BRANCH), lambda i: (i, 0)),
        _layer_spec(pool_w.shape, layer), _const_spec(pool_scale.shape), _const_spec(conv_w.shape),
        _layer_spec(glu_w.shape, layer), _const_spec(b_gate.shape), _layer_spec(w_br.shape, layer),
        _layer_spec(w_o.shape, layer),
    ]
    args = [x2, gate, g_post] + [proj] * 8 + [proj] * 4 + [proj] * 6 + [
        o_na, y_ssm, pool_w, pool_scale, conv_w, glu_w, b_gate, w_br, w_o]
    return pl.pallas_call(
        functools.partial(_merge_kernel, tm=tm, rows_per_seq=rows_per_seq),
        grid=(r // tm,),
        in_specs=in_specs,
        out_specs=pl.BlockSpec((tm, d), lambda i: (i, 0)),
        out_shape=jax.ShapeDtypeStruct((r, d), F32),
        scratch_shapes=[pltpu.VMEM((tm + 2 * HALO, BRANCH), F32)] * 3,
        compiler_params=_cparams(("arbitrary",)),
        name="merge",
    )(*args)


def _tile_rows(rows, pref):
    t = min(rows, pref)
    assert rows % t == 0
    return t


def kernel(x, c, ctx, c_ctx, w_mod, b_mod, g_pre, g_post, w_in, b_gate, na_rpb, pool_w, pool_scale,
           conv_w, ssm_a_re, ssm_a_im, ssm_log_dt, ssm_b_re, ssm_b_im, ssm_c_re, ssm_c_im, ssm_d,
           glu_w, w_br, w_o):
    batch, seq, d = x.shape
    n_ctx = ctx.shape[1]
    depth = w_mod.shape[0]
    assert d == D_MODEL and seq % (GRID_W * NA_WIN_ROWS) == 0 and batch + 1 <= 8
    assert n_ctx % HALO == 0 and w_in.shape[-1] == IN_TOTAL

    c8 = jnp.concatenate([c, c_ctx[None], jnp.zeros((7 - batch, d), F32)], axis=0)
    mod = _mod_call(c8, w_mod, b_mod)

    x2 = x.reshape(batch * seq, d)
    xc2 = ctx.reshape(batch * n_ctx, d)
    tm_in = _tile_rows(seq, INPROJ_ROW_TILE)
    tm_in_c = _tile_rows(batch * n_ctx, INPROJ_ROW_TILE)
    tm_mg = _tile_rows(seq, MERGE_ROW_TILE)
    tm_mg_c = _tile_rows(n_ctx, MERGE_ROW_TILE)
    tn_in = INPROJ_COL_TILE

    bias_tbls = jax.vmap(_na_bias_table)(na_rpb)
    c1s, ccs, decays = jax.vmap(_ssm_weights)(ssm_a_re, ssm_a_im, ssm_log_dt, ssm_b_re, ssm_b_im,
                                              ssm_c_re, ssm_c_im)
    if batch % SSM_BATCH_PER_STEP == 0:
        decays = jnp.tile(decays, (1, 1, 1, SSM_BATCH_PER_STEP, 1))
    dsks = jnp.tile(ssm_d.astype(F32).reshape(depth, SSM_TILES, 1, V7X_LANES), (1, 1, 1, SSM_CHUNK))
    col_scale = jnp.where(jnp.arange(IN_TOTAL) < COL_MERGE * BRANCH, 1.0, 0.5).astype(F32)
    w_in_bf = (w_in * col_scale).astype(BF16)
    pool_w_bf, glu_w_bf = pool_w.astype(BF16), glu_w.astype(BF16)
    w_br_bf, w_o_bf = w_br.astype(BF16), (0.5 * w_o).astype(BF16)

    for i in range(depth):
        with_ctx_out = i < depth - 1
        shift, scale, gate = (mod[i, :, k * d:(k + 1) * d] for k in range(3))
        lat = lambda v: v[:batch].reshape(batch, 1, d)
        cx = lambda v: v[batch:batch + 1].reshape(1, 1, d)
        gp = g_pre[i].reshape(1, d)
        gq = g_post[i].reshape(1, d)

        proj = _inproj_call(x2, lat(scale), lat(shift), gp, w_in_bf, i, seq, tm_in, tn_in)
        if with_ctx_out:
            projc = _inproj_call(xc2, cx(scale), cx(shift), gp, w_in_bf, i, batch * n_ctx, tm_in_c, tn_in)
            ctx_u_col = COL_SSM_U * BRANCH
        else:
            assert COL_V * BRANCH < tn_in and (COL_SSM_U * BRANCH) // tn_in == 2
            projc = _inproj_call(xc2, cx(scale), cx(shift), gp, w_in_bf, i, batch * n_ctx, tm_in_c, tn_in,
                                 col_tiles=(2, 2))
            ctx_u_col = COL_SSM_U * BRANCH - tn_in

        o_na = _na_call(proj, projc, bias_tbls, i, batch, seq, n_ctx)
        yc_ssm, y_ssm = _ssm_call(proj, projc, c1s, ccs, decays, dsks, i, batch, seq, n_ctx, ctx_u_col)

        mw = (pool_w_bf, pool_scale[i].reshape(1, BRANCH), conv_w[i], glu_w_bf,
              0.5 * b_gate[i].reshape(1, N_BRANCHES * d), w_br_bf, w_o_bf, i)
        x2 = _merge_call(x2, lat(gate), gq, proj, o_na, y_ssm, *mw, seq, tm_mg)
        if with_ctx_out:
            oc_na = _ctx_attn_call(projc, batch, n_ctx)
            gate_c = jnp.broadcast_to(cx(gate), (batch, 1, d))
            xc2 = _merge_call(xc2, gate_c, gq, projc, oc_na, yc_ssm, *mw, n_ctx, tm_mg_c)
    return x2.reshape(batch, seq, d)
```

```python
import functools

import jax
import jax.numpy as jnp
import numpy as np
from jax import lax
from jax.experimental import pallas as pl
from jax.experimental.pallas import tpu as pltpu

F32 = jnp.float32
BF16 = jnp.bfloat16

D_MODEL = 2048
GRID_W = 64
RMS_EPS = 1e-6
NEG_INF = -1e30
BRANCH = 512
N_BRANCHES = 4
NA_HEADS = 8
NA_HEAD_DIM = 64
NA_WIN_ROWS = 8
NA_WIN_COLS = 16
NA_GROUP_ROWS = 4
NA_KEY_ROWS = 12
POOL_WINDOWS = (2, 4, 8, 16)
POOL_GROUP_DIM = 128
SSM_GROUPS = 32
SSM_GROUP_DIM = 16
SSM_STATE = 64
IN_TOTAL = 12 * BRANCH + N_BRANCHES * D_MODEL
COL_Q, COL_K, COL_V, COL_NA_Z = 0, 1, 2, 3
COL_POOL_U, COL_POOL_Z = 4, 5
COL_CONV_X, COL_CONV_B, COL_CONV_C, COL_CONV_Z = 6, 7, 8, 9
COL_SSM_U, COL_SSM_Z = 10, 11
COL_MERGE = 12

V7X_LANES = 128
SSM_CHUNK = 8
SSM_LANE_GROUPS = V7X_LANES // SSM_GROUP_DIM
SSM_TILES = BRANCH // V7X_LANES
SSM_BATCH_PER_STEP = 2
HALO = 32
assert POOL_WINDOWS == (2, 4, 8, 16) and HALO >= 29 and HALO % 16 == 0
VMEM_LIMIT = 56 * 1024 * 1024
INPROJ_ROW_TILE = 1024
INPROJ_COL_TILE = 2048
MERGE_ROW_TILE = 256


def _cparams(sem):
    return pltpu.CompilerParams(dimension_semantics=sem, vmem_limit_bytes=VMEM_LIMIT)


def _const_spec(shape):
    nd = len(shape)
    return pl.BlockSpec(shape, lambda *_: (0,) * nd, pipeline_mode=pl.Buffered(1))


def _layer_spec(stacked_shape, layer):
    nd = len(stacked_shape) - 1
    return pl.BlockSpec((None,) + tuple(stacked_shape[1:]), lambda *_: (layer,) + (0,) * nd,
                        pipeline_mode=pl.Buffered(1))


def _mod_kernel(c_ref, w_ref, b_ref, o_ref):
    c = c_ref[...]
    a = (c * jax.nn.sigmoid(c)).astype(BF16)
    o_ref[0] = jnp.dot(a, w_ref[0].astype(BF16), preferred_element_type=F32) + b_ref[0]


def _mod_call(c8, w_mod, b_mod):
    depth, d, n3 = w_mod.shape
    tn = 512
    return pl.pallas_call(
        _mod_kernel,
        grid=(depth, n3 // tn),
        in_specs=[
            pl.BlockSpec((8, d), lambda l, j: (0, 0)),
            pl.BlockSpec((1, d, tn), lambda l, j: (l, 0, j)),
            pl.BlockSpec((1, 1, tn), lambda l, j: (l, 0, j)),
        ],
        out_specs=pl.BlockSpec((1, 8, tn), lambda l, j: (l, 0, j)),
        out_shape=jax.ShapeDtypeStruct((depth, 8, n3), F32),
        compiler_params=_cparams(("arbitrary", "arbitrary")),
        name="mod",
    )(c8, w_mod, b_mod.reshape(depth, 1, n3))


def _inproj_kernel(x_ref, scale_ref, shift_ref, g_ref, w_ref, o_ref, h_ref):
    @pl.when(pl.program_id(1) == 0)
    def _():
        x = x_ref[...]
        ms = jnp.mean(x * x, axis=-1, keepdims=True)
        gain = g_ref[...] * (1.0 + scale_ref[0])
        h_ref[...] = (x * lax.rsqrt(ms + RMS_EPS) * gain + shift_ref[0]).astype(BF16)

    o_ref[...] = jnp.dot(h_ref[...], w_ref[...], preferred_element_type=F32).astype(o_ref.dtype)


def _inproj_call(x2, scale, shift, g_pre, w_bf, layer, rows_per_seq, tm, tn, col_tiles=None):
    r, d = x2.shape
    n_col_tiles, col_tile_step = col_tiles if col_tiles else (w_bf.shape[2] // tn, 1)
    tiles_per_seq = rows_per_seq // tm
    return pl.pallas_call(
        _inproj_kernel,
        grid=(r // tm, n_col_tiles),
        in_specs=[
            pl.BlockSpec((tm, d), lambda i, j: (i, 0)),
            pl.BlockSpec((1, 1, d), lambda i, j: (i // tiles_per_seq, 0, 0)),
            pl.BlockSpec((1, 1, d), lambda i, j: (i // tiles_per_seq, 0, 0)),
            pl.BlockSpec((1, d), lambda i, j: (0, 0)),
            pl.BlockSpec((None, d, tn), lambda i, j: (layer, 0, j * col_tile_step)),
        ],
        out_specs=pl.BlockSpec((tm, tn), lambda i, j: (i, j)),
        out_shape=jax.ShapeDtypeStruct((r, n_col_tiles * tn), BF16),
        scratch_shapes=[pltpu.VMEM((tm, d), BF16)],
        compiler_params=_cparams(("arbitrary", "arbitrary")),
        name="inproj",
    )(x2, scale, shift, g_pre, w_bf)


def _head_mask(rows):
    lane = lax.broadcasted_iota(jnp.int32, (rows, V7X_LANES), 1)
    return lane < NA_HEAD_DIM


def _na_kernel(q_ref, k_ref, v_ref, kc_ref, vc_ref, t_ref, o_ref, bias_ref, *, grid_rows):
    @pl.when((pl.program_id(0) == 0) & (pl.program_id(1) == 0))
    def _():
        _na_expand_bias(t_ref, bias_ref)

    r0 = pl.program_id(1) * NA_GROUP_ROWS
    nq = NA_GROUP_ROWS * GRID_W
    first_half = _head_mask(nq)
    scale = NA_HEAD_DIM ** -0.5
    nt = (((1,), (1,)), ((), ()))
    gs = jnp.clip(r0 - NA_WIN_ROWS // 2, 0, grid_rows - NA_KEY_ROWS)
    cls = jnp.where(r0 == 0, 0, jnp.where(r0 == grid_rows - NA_GROUP_ROWS, 2, 1))
    k0 = pl.multiple_of(gs * GRID_W, GRID_W)
    def scores(hp):
        cs = slice(hp * V7X_LANES, (hp + 1) * V7X_LANES)
        qp = q_ref[:, cs] * jnp.asarray(scale, BF16)
        kp = k_ref[pl.ds(k0, NA_KEY_ROWS * GRID_W), cs]
        kcp = kc_ref[:, cs]
        zero = jnp.zeros_like(qp)
        q2 = jnp.concatenate([jnp.where(first_half, qp, zero), jnp.where(first_half, zero, qp)], axis=0)
        s_b = lax.dot_general(q2, kp, nt, preferred_element_type=F32) + bias_ref[cls, hp]
        s_c = lax.dot_general(q2, kcp, nt, preferred_element_type=F32)
        return s_b, s_c

    def attend(hp, s_b, s_c):
        cs = slice(hp * V7X_LANES, (hp + 1) * V7X_LANES)
        vp = v_ref[pl.ds(k0, NA_KEY_ROWS * GRID_W), cs]
        vcp = vc_ref[:, cs]
        m = jnp.maximum(jnp.max(s_b, axis=-1, keepdims=True), jnp.max(s_c, axis=-1, keepdims=True))
        p_b = jnp.exp(s_b - m)
        p_c = jnp.exp(s_c - m)
        denom = jnp.sum(p_b, axis=-1, keepdims=True) + jnp.sum(p_c, axis=-1, keepdims=True)
        o = (jnp.dot(p_b.astype(BF16), vp, preferred_element_type=F32)
             + jnp.dot(p_c.astype(BF16), vcp, preferred_element_type=F32)) / denom
        o_ref[:, cs] = jnp.where(first_half, o[0:nq], o[nq:]).astype(o_ref.dtype)

    n_pairs = NA_HEADS // 2
    sc = scores(0)
    for hp in range(n_pairs):
        nxt = scores(hp + 1) if hp + 1 < n_pairs else None
        attend(hp, *sc)
        sc = nxt


def _na_call(proj, projc, bias_tbls, layer, batch, seq, n_ctx):
    grid_rows = seq // GRID_W
    assert grid_rows % NA_GROUP_ROWS == 0 and grid_rows >= NA_KEY_ROWS
    tq = NA_GROUP_ROWS * GRID_W
    qb = seq // tq
    return pl.pallas_call(
        functools.partial(_na_kernel, grid_rows=grid_rows),
        grid=(batch, qb),
        in_specs=[
            pl.BlockSpec((tq, BRANCH), lambda b, r: (b * qb + r, COL_Q)),
            pl.BlockSpec((seq, BRANCH), lambda b, r: (b, COL_K)),
            pl.BlockSpec((seq, BRANCH), lambda b, r: (b, COL_V)),
            pl.BlockSpec((n_ctx, BRANCH), lambda b, r: (b, COL_K)),
            pl.BlockSpec((n_ctx, BRANCH), lambda b, r: (b, COL_V)),
            _layer_spec(bias_tbls.shape, layer),
        ],
        out_specs=pl.BlockSpec((tq, BRANCH), lambda b, r: (b * qb + r, 0)),
        out_shape=jax.ShapeDtypeStruct((batch * seq, BRANCH), BF16),
        scratch_shapes=[pltpu.VMEM((3, NA_HEADS // 2, 2 * tq, NA_KEY_ROWS * GRID_W), F32)],
        compiler_params=_cparams(("arbitrary", "arbitrary")),
        name="na_attn",
    )(proj, proj, proj, projc, projc, bias_tbls)


def _ctx_attn_kernel(q_ref, k_ref, v_ref, o_ref):
    n = q_ref.shape[0]
    first_half = _head_mask(n)
    scale = NA_HEAD_DIM ** -0.5
    nt = (((1,), (1,)), ((), ()))
    for hp in range(NA_HEADS // 2):
        cs = slice(hp * V7X_LANES, (hp + 1) * V7X_LANES)
        qp = q_ref[:, cs] * jnp.asarray(scale, BF16)
        kp = k_ref[:, cs]
        vp = v_ref[:, cs]
        zero = jnp.zeros_like(qp)
        q2 = jnp.concatenate([jnp.where(first_half, qp, zero), jnp.where(first_half, zero, qp)], axis=0)
        s = lax.dot_general(q2, kp, nt, preferred_element_type=F32)
        m = jnp.max(s, axis=-1, keepdims=True)
        p = jnp.exp(s - m)
        denom = jnp.sum(p, axis=-1, keepdims=True)
        o = jnp.dot(p.astype(BF16), vp, preferred_element_type=F32) / denom
        o_ref[:, cs] = jnp.where(first_half, o[0:n], o[n:]).astype(o_ref.dtype)


def _ctx_attn_call(projc, batch, n_ctx):
    return pl.pallas_call(
        _ctx_attn_kernel,
        grid=(batch,),
        in_specs=[
            pl.BlockSpec((n_ctx, BRANCH), lambda b: (b, COL_Q)),
            pl.BlockSpec((n_ctx, BRANCH), lambda b: (b, COL_K)),
            pl.BlockSpec((n_ctx, BRANCH), lambda b: (b, COL_V)),
        ],
        out_specs=pl.BlockSpec((n_ctx, BRANCH), lambda b: (b, 0)),
        out_shape=jax.ShapeDtypeStruct((batch * n_ctx, BRANCH), BF16),
        compiler_params=_cparams(("arbitrary",)),
        name="ctx_attn",
    )(projc, projc, projc)


def _na_bias_table(rpb):
    col = np.arange(GRID_W)
    col_start = np.clip(col - NA_WIN_COLS // 2, 0, GRID_W - NA_WIN_COLS)
    in_win = (col[None, :] >= col_start[:, None]) & (col[None, :] < col_start[:, None] + NA_WIN_COLS)
    dcol = np.clip(col[None, :] - col[:, None] + (NA_WIN_COLS - 1), 0, 2 * NA_WIN_COLS - 2)
    onehot = (dcol[..., None] == np.arange(2 * NA_WIN_COLS - 1)).astype(np.float32)
    t = jnp.einsum('hdc,qwc->hqdw', rpb.astype(F32), onehot, precision=lax.Precision.HIGHEST)
    t = jnp.where(in_win[None, :, None, :], t, NEG_INF)
    t = t.transpose(0, 2, 1, 3)
    return jnp.concatenate([t, t], axis=-1)


def _na_band_rows(cls, i):
    lo = (0, i, NA_KEY_ROWS - NA_WIN_ROWS)[cls]
    off = (NA_WIN_ROWS - 1 - i, NA_WIN_ROWS // 2 - 1 - i, NA_WIN_ROWS // 2 - 1 - lo - i)[cls]
    return lo, off


def _na_expand_bias(t_ref, bias_ref):
    lane = lax.broadcasted_iota(jnp.int32, (GRID_W, V7X_LANES), 1)
    left = lane < GRID_W
    neg = jnp.full((GRID_W, V7X_LANES), NEG_INF, F32)
    for cls in range(3):
        for hp in range(NA_HEADS // 2):
            for e in range(2):
                for i in range(NA_GROUP_ROWS):
                    lo, off = _na_band_rows(cls, i)
                    r0 = (e * NA_GROUP_ROWS + i) * GRID_W
                    for kt in range(NA_KEY_ROWS // 2):
                        halves = []
                        for kw in (2 * kt, 2 * kt + 1):
                            inside = lo <= kw < lo + NA_WIN_ROWS
                            halves.append(t_ref[2 * hp + e, kw + off] if inside else neg)
                        bias_ref[cls, hp, r0:r0 + GRID_W, kt * V7X_LANES:(kt + 1) * V7X_LANES] = (
                            jnp.where(left, halves[0], halves[1]))


def _cmul(ar, ai, br, bi):
    return ar * br - ai * bi, ar * bi + ai * br


def _ssm_weights(a_re, a_im, log_dt, b_re, b_im, c_re, c_im):
    t_len = SSM_CHUNK
    p, hg = SSM_STATE, SSM_GROUP_DIM
    lg = SSM_LANE_GROUPS
    kf, ef, mf, dec = [], [], [], []
    for d in range(2):
        ar, ai = a_re[d].astype(F32), a_im[d].astype(F32)
        dt = jnp.exp(log_dt[d].astype(F32))[:, None]
        mag = jnp.exp(ar * dt)
        abr, abi = mag * jnp.cos(ai * dt), mag * jnp.sin(ai * dt)
        den = ar * ar + ai * ai
        nr, ni = abr - 1.0, abi
        fr = (nr * ar + ni * ai) / den
        fi = (ni * ar - nr * ai) / den
        bbr, bbi = _cmul(fr[..., None], fi[..., None], b_re[d].astype(F32), b_im[d].astype(F32))
        pws = [(jnp.ones_like(abr), jnp.zeros_like(abr))]
        for _ in range(t_len):
            pws.append(_cmul(pws[-1][0], pws[-1][1], abr, abi))
        pr = jnp.stack([x[0] for x in pws])
        pi = jnp.stack([x[1] for x in pws])
        cr, ci = c_re[d].astype(F32), c_im[d].astype(F32)
        mr, mi = _cmul(cr[None], ci[None], pr[:t_len, :, None, :], pi[:t_len, :, None, :])
        kf.append(jnp.einsum('tghp,gpk->tghk', mr, bbr) - jnp.einsum('tghp,gpk->tghk', mi, bbi))
        e_pw = [pws[t_len - 1 - t] if d == 0 else pws[t] for t in range(t_len)]
        per, pei = jnp.stack([x[0] for x in e_pw]), jnp.stack([x[1] for x in e_pw])
        er, ei = _cmul(per[..., None], pei[..., None], bbr[None], bbi[None])
        ef.append((er, ei))
        c_pw = [pws[t + 1] if d == 0 else pws[t_len - t] for t in range(t_len)]
        pcr, pci = jnp.stack([x[0] for x in c_pw]), jnp.stack([x[1] for x in c_pw])
        m2r, m2i = _cmul(cr[None], ci[None], pcr[:, :, None, :], pci[:, :, None, :])
        mf.append((m2r, m2i))
        dec.append((pr[t_len], pi[t_len]))
    kc = [k.reshape(t_len, SSM_TILES, lg, hg, hg).transpose(0, 1, 4, 2, 3).reshape(
        t_len, SSM_TILES, hg, V7X_LANES) for k in kf]
    z = ([kc[1][t_len - 1 - i] for i in range(t_len - 1)] + [kc[0][0] + kc[1][0]]
         + [kc[0][i] for i in range(1, t_len)])
    ktoe = jnp.stack([jnp.stack(z[t_len - 1 - tp:2 * t_len - 1 - tp]) for tp in range(t_len)])
    parts = [ktoe.transpose(2, 0, 3, 1, 4).reshape(SSM_TILES, t_len, hg, t_len * V7X_LANES)]
    for d in range(2):
        for comp in ef[d]:
            parts.append(comp.reshape(t_len, SSM_TILES, lg, p, hg).transpose(1, 0, 4, 2, 3).reshape(
                SSM_TILES, t_len, hg, lg * p))
    c1 = jnp.concatenate(parts, axis=-1)
    rows = []
    for d in range(2):
        m2r, m2i = mf[d]
        for comp in (m2r, -m2i):
            rows.append(comp.reshape(t_len, SSM_TILES, lg, hg, p).transpose(1, 4, 0, 2, 3).reshape(
                SSM_TILES, p, t_len * V7X_LANES))
    cc = jnp.stack(rows, axis=1)
    decay = jnp.stack([x.reshape(SSM_TILES, SSM_TILES, V7X_LANES) for d in range(2) for x in dec[d]], axis=1)
    return c1.astype(BF16), cc.astype(BF16), decay


def _ssm_kernel(uc_ref, u_ref, c1_ref, cc_ref, dec_ref, dsk_ref, yc_ref, y_ref,
                w1_ref, wc_ref, uf_ref, lhs_ref, yi_ref, es_ref, *, n_ctx, seq, nb):
    t_len = SSM_CHUNK
    nc, lc = n_ctx // t_len, seq // t_len
    nj = nc + lc
    rows_b = n_ctx + seq
    kdim = t_len * V7X_LANES
    sw = SSM_LANE_GROUPS * SSM_STATE
    ntile = sw // V7X_LANES
    srows = nb * ntile
    hg = SSM_GROUP_DIM

    @pl.when(pl.program_id(1) == 0)
    def _():
        lane1 = lax.broadcasted_iota(jnp.int32, (1, kdim + 4 * sw), 1)
        grp1 = jnp.where(lane1 < kdim, (lane1 % V7X_LANES) // hg, ((lane1 - kdim) % sw) // SSM_STATE)
        lanec = lax.broadcasted_iota(jnp.int32, (1, kdim), 1)
        grpc = (lanec % V7X_LANES) // hg
        for t in range(t_len):
            row = c1_ref[0, t]
            for g in range(SSM_LANE_GROUPS):
                r0 = (t * SSM_LANE_GROUPS + g) * hg
                w1_ref[r0:r0 + hg, :] = jnp.where(grp1 == g, row, jnp.zeros_like(row))
        for part in range(4):
            blk = cc_ref[0, part]
            for g in range(SSM_LANE_GROUPS):
                r0 = part * sw + g * SSM_STATE
                wc_ref[r0:r0 + SSM_STATE, :] = jnp.where(grpc == g, blk, jnp.zeros_like(blk))

    for bl in range(nb):
        uf_ref[bl * rows_b:bl * rows_b + n_ctx, :] = uc_ref[bl * n_ctx:(bl + 1) * n_ctx, :].astype(F32)
        uf_ref[bl * rows_b + n_ctx:(bl + 1) * rows_b, :] = u_ref[bl * seq:(bl + 1) * seq, :].astype(F32)
    for t in range(t_len):
        lhs_ref[:, t * V7X_LANES:(t + 1) * V7X_LANES] = uf_ref[pl.ds(t, nb * nj, stride=t_len), :]
    lhs = lhs_ref[...].astype(BF16)
    yi_ref[...] = jnp.dot(lhs, w1_ref[:, 0:kdim], preferred_element_type=F32) + lhs_ref[...] * dsk_ref[0]
    for c in range(4):
        e = jnp.dot(lhs, w1_ref[:, kdim + c * sw:kdim + (c + 1) * sw], preferred_element_type=F32)
        for bl in range(nb):
            for k in range(ntile):
                es_ref[c, pl.ds(bl * ntile + k, nj, stride=srows), :] = (
                    e[bl * nj:(bl + 1) * nj, k * V7X_LANES:(k + 1) * V7X_LANES])

    afr, afi, abr, abi = dec_ref[0, 0], dec_ref[0, 1], dec_ref[0, 2], dec_ref[0, 3]

    def step(k, st):
        sfr, sfi, sbr, sbi = st
        jb = jnp.where(k < nc, nc - 1 - k, nj - 1 - (k - nc))
        rf = pl.ds(pl.multiple_of(k * srows, srows), srows)
        rb = pl.ds(pl.multiple_of(jb * srows, srows), srows)
        efr, efi = es_ref[0, rf, :], es_ref[1, rf, :]
        ebr, ebi = es_ref[2, rb, :], es_ref[3, rb, :]
        es_ref[0, rf, :] = sfr
        es_ref[1, rf, :] = sfi
        es_ref[2, rb, :] = sbr
        es_ref[3, rb, :] = sbi
        nfr, nfi = _cmul(afr, afi, sfr, sfi)
        nbr, nbi = _cmul(abr, abi, sbr, sbi)
        return nfr + efr, nfi + efi, nbr + ebr, nbi + ebi

    z = jnp.zeros((srows, V7X_LANES), F32)
    lax.fori_loop(0, nj, step, (z, z, z, z))

    carried = jnp.concatenate([
        jnp.concatenate([es_ref[c, pl.ds(bl * ntile + k, nj, stride=srows), :]
                         for c in range(4) for k in range(ntile)], axis=1)
        for bl in range(nb)], axis=0)
    yi_ref[...] += jnp.dot(carried.astype(BF16), wc_ref[...], preferred_element_type=F32)
    for t in range(t_len):
        uf_ref[pl.ds(t, nb * nj, stride=t_len), :] = yi_ref[:, t * V7X_LANES:(t + 1) * V7X_LANES]
    for bl in range(nb):
        yc_ref[bl * n_ctx:(bl + 1) * n_ctx, :] = uf_ref[bl * rows_b:bl * rows_b + n_ctx, :]
        y_ref[bl * seq:(bl + 1) * seq, :] = uf_ref[bl * rows_b + n_ctx:(bl + 1) * rows_b, :]


def _ssm_call(proj, projc, c1, cc, decay, dsk_tiled, layer, batch, seq, n_ctx, ctx_col):
    t_len = SSM_CHUNK
    nj = (n_ctx + seq) // t_len
    kdim = t_len * V7X_LANES
    sw4 = 4 * SSM_LANE_GROUPS * SSM_STATE
    ntile = sw4 // 4 // V7X_LANES
    col0 = COL_SSM_U * BRANCH // V7X_LANES
    colc = ctx_col // V7X_LANES
    nb = SSM_BATCH_PER_STEP if batch % SSM_BATCH_PER_STEP == 0 else 1
    assert decay.shape[-2] == nb * ntile
    return pl.pallas_call(
        functools.partial(_ssm_kernel, n_ctx=n_ctx, seq=seq, nb=nb),
        grid=(SSM_TILES, batch // nb),
        in_specs=[
            pl.BlockSpec((nb * n_ctx, V7X_LANES), lambda s, b: (b, colc + s)),
            pl.BlockSpec((nb * seq, V7X_LANES), lambda s, b: (b, col0 + s)),
            pl.BlockSpec((None, 1, t_len, SSM_GROUP_DIM, kdim + sw4), lambda s, b: (layer, s, 0, 0, 0)),
            pl.BlockSpec((None, 1, 4, SSM_STATE, kdim), lambda s, b: (layer, s, 0, 0, 0)),
            pl.BlockSpec((None, 1, 4, nb * ntile, V7X_LANES), lambda s, b: (layer, s, 0, 0, 0)),
            pl.BlockSpec((None, 1, 1, kdim), lambda s, b: (layer, s, 0, 0)),
        ],
        out_specs=[
            pl.BlockSpec((nb * n_ctx, V7X_LANES), lambda s, b: (b, s)),
            pl.BlockSpec((nb * seq, V7X_LANES), lambda s, b: (b, s)),
        ],
        out_shape=[
            jax.ShapeDtypeStruct((batch * n_ctx, BRANCH), F32),
            jax.ShapeDtypeStruct((batch * seq, BRANCH), F32),
        ],
        scratch_shapes=[
            pltpu.VMEM((kdim, kdim + sw4), BF16),
            pltpu.VMEM((sw4, kdim), BF16),
            pltpu.VMEM((nb * (n_ctx + seq), V7X_LANES), F32),
            pltpu.VMEM((nb * nj, kdim), F32),
            pltpu.VMEM((nb * nj, kdim), F32),
            pltpu.VMEM((4, nb * nj * ntile, V7X_LANES), F32),
        ],
        compiler_params=_cparams(("arbitrary", "arbitrary")),
        name="ssm",
    )(projc, proj, c1, cc, decay, dsk_tiled)


def _merge_kernel(x_ref, gate_ref, gpost_ref,
                  naz_ref, pu_ref, pz_ref, cx_ref, cb_ref, cc_ref, cz_ref, sz_ref,
                  lg0_ref, lg1_ref, lg2_ref, lg3_ref,
                  pu_prev_ref, pu_next_ref, cx_prev_ref, cx_next_ref, cc_prev_ref, cc_next_ref,
                  ona_ref, yssm_ref,
                  poolw_ref, pscale_ref, convw_ref, gluw_ref, bgate_ref, wbr_ref, wo_ref,
                  o_ref, pad_ref, s2_ref, s4_ref, *, tm, rows_per_seq):
    i = pl.program_id(0)
    tiles_per_seq = rows_per_seq // tm
    ti = i % tiles_per_seq
    has_prev = ti > 0
    has_next = ti < tiles_per_seq - 1
    t0 = ti * tm
    pos = t0 + lax.broadcasted_iota(jnp.int32, (tm, 1), 0)

    pad_ref[0:HALO, :] = jnp.where(has_prev, pu_prev_ref[...].astype(F32), 0.0)
    pad_ref[HALO:HALO + tm, :] = pu_ref[...].astype(F32)
    pad_ref[HALO + tm:2 * HALO + tm, :] = jnp.where(has_next, pu_next_ref[...].astype(F32), 0.0)
    base = HALO - max(POOL_WINDOWS) // 2
    n8, n4, n2 = tm + 16, tm + 24, tm + 32
    mixed = []
    for gi, w in enumerate(POOL_WINDOWS):
        cs = slice(gi * POOL_GROUP_DIM, (gi + 1) * POOL_GROUP_DIM)
        half = pad_ref
        if w >= 4:
            s2_ref[base:base + n2, cs] = pad_ref[base:base + n2, cs] + pad_ref[base + 1:base + 1 + n2, cs]
            half = s2_ref
        if w >= 8:
            s4_ref[base:base + n4, cs] = s2_ref[base:base + n4, cs] + s2_ref[base + 2:base + 2 + n4, cs]
            half = s4_ref
        if w >= 16:
            s2_ref[base:base + n8, cs] = s4_ref[base:base + n8, cs] + s4_ref[base + 4:base + 4 + n8, cs]
            half = s2_ref
        acc = half[HALO - w // 2:HALO - w // 2 + tm, cs] + half[HALO:HALO + tm, cs]
        lo = jnp.maximum(pos - w // 2, 0)
        hi = jnp.minimum(pos + w - w // 2, rows_per_seq)
        cnt = (hi - lo).astype(F32)
        pooled = acc / cnt - pad_ref[HALO:HALO + tm, cs]
        mixed.append(jnp.dot(pooled.astype(BF16), poolw_ref[gi], preferred_element_type=F32))
    o_pool = jnp.concatenate(mixed, axis=-1) * pscale_ref[...]

    pad_ref[0:HALO, :] = jnp.where(
        has_prev, cc_prev_ref[...].astype(F32) * cx_prev_ref[...].astype(F32), 0.0)
    pad_ref[HALO:HALO + tm, :] = cc_ref[...].astype(F32) * cx_ref[...].astype(F32)
    pad_ref[HALO + tm:2 * HALO + tm, :] = jnp.where(
        has_next, cc_next_ref[...].astype(F32) * cx_next_ref[...].astype(F32), 0.0)
    cw = convw_ref[...]
    conv = (pad_ref[HALO - 1:HALO - 1 + tm, :] * cw[0:1] + pad_ref[HALO:HALO + tm, :] * cw[1:2]
            + pad_ref[HALO + 1:HALO + 1 + tm, :] * cw[2:3])
    o_conv = cb_ref[...].astype(F32) * conv

    g = jax.nn.gelu(yssm_ref[...]).astype(BF16)
    gg = jnp.dot(g, gluw_ref[...], preferred_element_type=F32)
    ga = 0.5 * gg[:, 0:BRANCH]
    o_ssm = ga + ga * jnp.tanh(0.5 * gg[:, BRANCH:2 * BRANCH])

    outs = (ona_ref[...].astype(F32), o_pool, o_conv, o_ssm)
    zs = (naz_ref, pz_ref, cz_ref, sz_ref)
    lgs = (lg0_ref, lg1_ref, lg2_ref, lg3_ref)
    merged = None
    for bi in range(N_BRANCHES):
        hz = 0.5 * zs[bi][...].astype(F32)
        a = (outs[bi] * (hz + hz * jnp.tanh(hz))).astype(BF16)
        br = jnp.dot(a, wbr_ref[bi * BRANCH:(bi + 1) * BRANCH, :], preferred_element_type=F32)
        th = jnp.tanh(lgs[bi][...].astype(F32) + bgate_ref[:, bi * D_MODEL:(bi + 1) * D_MODEL])
        term = (1.0 + th) * br
        merged = term if merged is None else merged + term
    y = jnp.dot(merged.astype(BF16), wo_ref[...], preferred_element_type=F32)
    ms = jnp.mean(y * y, axis=-1, keepdims=True)
    gain = gpost_ref[...] * gate_ref[0]
    o_ref[...] = x_ref[...] + y * lax.rsqrt(ms + RMS_EPS) * gain


def _merge_call(x2, gate, g_post, proj, o_na, y_ssm, pool_w, pool_scale, conv_w, glu_w, b_gate,
                w_br, w_o, layer, rows_per_seq, tm):
    r, d = x2.shape
    tiles_per_seq = rows_per_seq // tm
    hb = tm // HALO
    n_halo_blocks = r // HALO

    def col(c):
        return pl.BlockSpec((tm, BRANCH), lambda i: (i, c))

    def lg(c):
        return pl.BlockSpec((tm, D_MODEL), lambda i: (i, COL_MERGE * BRANCH // D_MODEL + c))

    def prev(c):
        return pl.BlockSpec((HALO, BRANCH), lambda i: (jnp.maximum(i * hb - 1, 0), c))

    def nxt(c):
        return pl.BlockSpec((HALO, BRANCH), lambda i: (jnp.minimum((i + 1) * hb, n_halo_blocks - 1), c))

    in_specs = [
        pl.BlockSpec((tm, d), lambda i: (i, 0)),
        pl.BlockSpec((1, 1, d), lambda i: (i // tiles_per_seq, 0, 0)),
        pl.BlockSpec((1, d), lambda i: (0, 0)),
        col(COL_NA_Z), col(COL_POOL_U), col(COL_POOL_Z), col(COL_CONV_X), col(COL_CONV_B),
        col(COL_CONV_C), col(COL_CONV_Z), col(COL_SSM_Z),
        lg(0), lg(1), lg(2), lg(3),
        prev(COL_POOL_U), nxt(COL_POOL_U), prev(COL_CONV_X), nxt(COL_CONV_X),
        prev(COL_CONV_C), nxt(COL_CONV_C),
        pl.BlockSpec((tm, BRANCH), lambda i: (i, 0)),
        pl.BlockSpec((tm, BRANCH), lambda i: (i, 0)),
        _layer_spec(pool_w.shape, layer), _const_spec(pool_scale.shape), _const_spec(conv_w.shape),
        _layer_spec(glu_w.shape, layer), _const_spec(b_gate.shape), _layer_spec(w_br.shape, layer),
        _layer_spec(w_o.shape, layer),
    ]
    args = [x2, gate, g_post] + [proj] * 8 + [proj] * 4 + [proj] * 6 + [
        o_na, y_ssm, pool_w, pool_scale, conv_w, glu_w, b_gate, w_br, w_o]
    return pl.pallas_call(
        functools.partial(_merge_kernel, tm=tm, rows_per_seq=rows_per_seq),
        grid=(r // tm,),
        in_specs=in_specs,
        out_specs=pl.BlockSpec((tm, d), lambda i: (i, 0)),
        out_shape=jax.ShapeDtypeStruct((r, d), F32),
        scratch_shapes=[pltpu.VMEM((tm + 2 * HALO, BRANCH), F32)] * 3,
        compiler_params=_cparams(("arbitrary",)),
        name="merge",
    )(*args)


def _tile_rows(rows, pref):
    t = min(rows, pref)
    assert rows % t == 0
    return t


def kernel(x, c, ctx, c_ctx, w_mod, b_mod, g_pre, g_post, w_in, b_gate, na_rpb, pool_w, pool_scale,
           conv_w, ssm_a_re, ssm_a_im, ssm_log_dt, ssm_b_re, ssm_b_im, ssm_c_re, ssm_c_im, ssm_d,
           glu_w, w_br, w_o):
    batch, seq, d = x.shape
    n_ctx = ctx.shape[1]
    depth = w_mod.shape[0]
    assert d == D_MODEL and seq % (GRID_W * NA_WIN_ROWS) == 0 and batch + 1 <= 8
    assert n_ctx % HALO == 0 and w_in.shape[-1] == IN_TOTAL

    c8 = jnp.concatenate([c, c_ctx[None], jnp.zeros((7 - batch, d), F32)], axis=0)
    mod = _mod_call(c8, w_mod, b_mod)

    x2 = x.reshape(batch * seq, d)
    xc2 = ctx.reshape(batch * n_ctx, d)
    tm_in = _tile_rows(seq, INPROJ_ROW_TILE)
    tm_in_c = _tile_rows(batch * n_ctx, INPROJ_ROW_TILE)
    tm_mg = _tile_rows(seq, MERGE_ROW_TILE)
    tm_mg_c = _tile_rows(n_ctx, MERGE_ROW_TILE)
    tn_in = INPROJ_COL_TILE

    bias_tbls = jax.vmap(_na_bias_table)(na_rpb)
    c1s, ccs, decays = jax.vmap(_ssm_weights)(ssm_a_re, ssm_a_im, ssm_log_dt, ssm_b_re, ssm_b_im,
                                              ssm_c_re, ssm_c_im)
    if batch % SSM_BATCH_PER_STEP == 0:
        decays = jnp.tile(decays, (1, 1, 1, SSM_BATCH_PER_STEP, 1))
    dsks = jnp.tile(ssm_d.astype(F32).reshape(depth, SSM_TILES, 1, V7X_LANES), (1, 1, 1, SSM_CHUNK))
    col_scale = jnp.where(jnp.arange(IN_TOTAL) < COL_MERGE * BRANCH, 1.0, 0.5).astype(F32)
    w_in_bf = (w_in * col_scale).astype(BF16)
    pool_w_bf, glu_w_bf = pool_w.astype(BF16), glu_w.astype(BF16)
    w_br_bf, w_o_bf = w_br.astype(BF16), (0.5 * w_o).astype(BF16)

    for i in range(depth):
        with_ctx_out = i < depth - 1
        shift, scale, gate = (mod[i, :, k * d:(k + 1) * d] for k in range(3))
        lat = lambda v: v[:batch].reshape(batch, 1, d)
        cx = lambda v: v[batch:batch + 1].reshape(1, 1, d)
        gp = g_pre[i].reshape(1, d)
        gq = g_post[i].reshape(1, d)

        proj = _inproj_call(x2, lat(scale), lat(shift), gp, w_in_bf, i, seq, tm_in, tn_in)
        if with_ctx_out:
            projc = _inproj_call(xc2, cx(scale), cx(shift), gp, w_in_bf, i, batch * n_ctx, tm_in_c, tn_in)
            ctx_u_col = COL_SSM_U * BRANCH
        else:
            assert COL_V * BRANCH < tn_in and (COL_SSM_U * BRANCH) // tn_in == 2
            projc = _inproj_call(xc2, cx(scale), cx(shift), gp, w_in_bf, i, batch * n_ctx, tm_in_c, tn_in,
                                 col_tiles=(2, 2))
            ctx_u_col = COL_SSM_U * BRANCH - tn_in

        o_na = _na_call(proj, projc, bias_tbls, i, batch, seq, n_ctx)
        yc_ssm, y_ssm = _ssm_call(proj, projc, c1s, ccs, decays, dsks, i, batch, seq, n_ctx, ctx_u_col)

        mw = (pool_w_bf, pool_scale[i].reshape(1, BRANCH), conv_w[i], glu_w_bf,
              0.5 * b_gate[i].reshape(1, N_BRANCHES * d), w_br_bf, w_o_bf, i)
        x2 = _merge_call(x2, lat(gate), gq, proj, o_na, y_ssm, *mw, seq, tm_mg)
        if with_ctx_out:
            oc_na = _ctx_attn_call(projc, batch, n_ctx)
            gate_c = jnp.broadcast_to(cx(gate), (batch, 1, d))
            xc2 = _merge_call(xc2, gate_c, gq, projc, oc_na, yc_ssm, *mw, n_ctx, tm_mg_c)
    return x2.reshape(batch, seq, d)
```

```python
import functools

import jax
import jax.numpy as jnp
import numpy as np
from jax import lax
from jax.experimental import pallas as pl
from jax.experimental.pallas import tpu as pltpu

F32 = jnp.float32
BF16 = jnp.bfloat16

D_MODEL = 2048
GRID_W = 64
RMS_EPS = 1e-6
NEG_INF = -1e30
LOG2_E = 1.4426950408889634
BRANCH = 512
N_BRANCHES = 4
NA_HEADS = 8
NA_HEAD_DIM = 64
NA_WIN_ROWS = 8
NA_WIN_COLS = 16
NA_GROUP_ROWS = 4
NA_KEY_ROWS = 12
POOL_WINDOWS = (2, 4, 8, 16)
POOL_GROUP_DIM = 128
SSM_GROUPS = 32
SSM_GROUP_DIM = 16
SSM_STATE = 64
IN_TOTAL = 12 * BRANCH + N_BRANCHES * D_MODEL
COL_Q, COL_K, COL_V, COL_NA_Z = 0, 1, 2, 3
COL_POOL_U, COL_POOL_Z = 4, 5
COL_CONV_X, COL_CONV_B, COL_CONV_C, COL_CONV_Z = 6, 7, 8, 9
COL_SSM_U, COL_SSM_Z = 10, 11
COL_MERGE = 12

V7X_LANES = 128
SSM_CHUNK = 8
SSM_LANE_GROUPS = V7X_LANES // SSM_GROUP_DIM
SSM_TILES = BRANCH // V7X_LANES
SSM_BATCH_PER_STEP = 2
HALO = 32
assert POOL_WINDOWS == (2, 4, 8, 16) and HALO >= 29 and HALO % 16 == 0
VMEM_LIMIT = 56 * 1024 * 1024
INPROJ_ROW_TILE = 1024
INPROJ_COL_TILE = 2048
MERGE_ROW_TILE = 256


def _cparams(sem):
    return pltpu.CompilerParams(dimension_semantics=sem, vmem_limit_bytes=VMEM_LIMIT)


def _const_spec(shape):
    nd = len(shape)
    return pl.BlockSpec(shape, lambda *_: (0,) * nd, pipeline_mode=pl.Buffered(1))


def _layer_spec(stacked_shape, layer):
    nd = len(stacked_shape) - 1
    return pl.BlockSpec((None,) + tuple(stacked_shape[1:]), lambda *_: (layer,) + (0,) * nd,
                        pipeline_mode=pl.Buffered(1))


def _mod_kernel(c_ref, w_ref, b_ref, o_ref):
    c = c_ref[...]
    a = (c * jax.nn.sigmoid(c)).astype(BF16)
    o_ref[0] = jnp.dot(a, w_ref[0].astype(BF16), preferred_element_type=F32) + b_ref[0]


def _mod_call(c8, w_mod, b_mod):
    depth, d, n3 = w_mod.shape
    tn = 512
    return pl.pallas_call(
        _mod_kernel,
        grid=(depth, n3 // tn),
        in_specs=[
            pl.BlockSpec((8, d), lambda l, j: (0, 0)),
            pl.BlockSpec((1, d, tn), lambda l, j: (l, 0, j)),
            pl.BlockSpec((1, 1, tn), lambda l, j: (l, 0, j)),
        ],
        out_specs=pl.BlockSpec((1, 8, tn), lambda l, j: (l, 0, j)),
        out_shape=jax.ShapeDtypeStruct((depth, 8, n3), F32),
        compiler_params=_cparams(("arbitrary", "arbitrary")),
        name="mod",
    )(c8, w_mod, b_mod.reshape(depth, 1, n3))


def _inproj_kernel(x_ref, scale_ref, shift_ref, g_ref, w_ref, o_ref, h_ref):
    @pl.when(pl.program_id(1) == 0)
    def _():
        x = x_ref[...]
        ms = jnp.mean(x * x, axis=-1, keepdims=True)
        gain = g_ref[...] * (1.0 + scale_ref[0])
        h_ref[...] = (x * lax.rsqrt(ms + RMS_EPS) * gain + shift_ref[0]).astype(BF16)

    o_ref[...] = jnp.dot(h_ref[...], w_ref[...], preferred_element_type=F32).astype(o_ref.dtype)


def _inproj_call(x2, scale, shift, g_pre, w_bf, layer, rows_per_seq, tm, tn, col_tiles=None):
    r, d = x2.shape
    n_col_tiles, col_tile_step = col_tiles if col_tiles else (w_bf.shape[2] // tn, 1)
    tiles_per_seq = rows_per_seq // tm
    return pl.pallas_call(
        _inproj_kernel,
        grid=(r // tm, n_col_tiles),
        in_specs=[
            pl.BlockSpec((tm, d), lambda i, j: (i, 0)),
            pl.BlockSpec((1, 1, d), lambda i, j: (i // tiles_per_seq, 0, 0)),
            pl.BlockSpec((1, 1, d), lambda i, j: (i // tiles_per_seq, 0, 0)),
            pl.BlockSpec((1, d), lambda i, j: (0, 0)),
            pl.BlockSpec((None, d, tn), lambda i, j: (layer, 0, j * col_tile_step)),
        ],
        out_specs=pl.BlockSpec((tm, tn), lambda i, j: (i, j)),
        out_shape=jax.ShapeDtypeStruct((r, n_col_tiles * tn), BF16),
        scratch_shapes=[pltpu.VMEM((tm, d), BF16)],
        compiler_params=_cparams(("arbitrary", "arbitrary")),
        name="inproj",
    )(x2, scale, shift, g_pre, w_bf)


def _head_mask(rows):
    lane = lax.broadcasted_iota(jnp.int32, (rows, V7X_LANES), 1)
    return lane < NA_HEAD_DIM


def _na_kernel(q_ref, k_ref, v_ref, kc_ref, vc_ref, t_ref, o_ref, bias_ref, *, grid_rows):
    @pl.when((pl.program_id(0) == 0) & (pl.program_id(1) == 0))
    def _():
        _na_expand_bias(t_ref, bias_ref)

    r0 = pl.program_id(1) * NA_GROUP_ROWS
    nq = NA_GROUP_ROWS * GRID_W
    first_half = _head_mask(nq)
    scale = NA_HEAD_DIM ** -0.5 * LOG2_E
    nt = (((1,), (1,)), ((), ()))
    gs = jnp.clip(r0 - NA_WIN_ROWS // 2, 0, grid_rows - NA_KEY_ROWS)
    cls = jnp.where(r0 == 0, 0, jnp.where(r0 == grid_rows - NA_GROUP_ROWS, 2, 1))
    k0 = pl.multiple_of(gs * GRID_W, GRID_W)
    def scores(hp):
        cs = slice(hp * V7X_LANES, (hp + 1) * V7X_LANES)
        qp = q_ref[:, cs] * jnp.asarray(scale, BF16)
        kp = k_ref[pl.ds(k0, NA_KEY_ROWS * GRID_W), cs]
        kcp = kc_ref[:, cs]
        zero = jnp.zeros_like(qp)
        q2 = jnp.concatenate([jnp.where(first_half, qp, zero), jnp.where(first_half, zero, qp)], axis=0)
        s_b = lax.dot_general(q2, kp, nt, preferred_element_type=F32) + bias_ref[cls, hp]
        s_c = lax.dot_general(q2, kcp, nt, preferred_element_type=F32)
        return s_b, s_c

    def attend(hp, s_b, s_c):
        cs = slice(hp * V7X_LANES, (hp + 1) * V7X_LANES)
        vp = v_ref[pl.ds(k0, NA_KEY_ROWS * GRID_W), cs]
        vcp = vc_ref[:, cs]
        m = jnp.maximum(jnp.max(s_b, axis=-1, keepdims=True), jnp.max(s_c, axis=-1, keepdims=True))
        p_b = jnp.exp2(s_b - m)
        p_c = jnp.exp2(s_c - m)
        denom = jnp.sum(p_b, axis=-1, keepdims=True) + jnp.sum(p_c, axis=-1, keepdims=True)
        o = (jnp.dot(p_b.astype(BF16), vp, preferred_element_type=F32)
             + jnp.dot(p_c.astype(BF16), vcp, preferred_element_type=F32)) / denom
        o_ref[:, cs] = jnp.where(first_half, o[0:nq], o[nq:]).astype(o_ref.dtype)

    n_pairs = NA_HEADS // 2
    sc = scores(0)
    for hp in range(n_pairs):
        nxt = scores(hp + 1) if hp + 1 < n_pairs else None
        attend(hp, *sc)
        sc = nxt


def _na_call(proj, projc, bias_tbls, layer, batch, seq, n_ctx):
    grid_rows = seq // GRID_W
    assert grid_rows % NA_GROUP_ROWS == 0 and grid_rows >= NA_KEY_ROWS
    tq = NA_GROUP_ROWS * GRID_W
    qb = seq // tq
    return pl.pallas_call(
        functools.partial(_na_kernel, grid_rows=grid_rows),
        grid=(batch, qb),
        in_specs=[
            pl.BlockSpec((tq, BRANCH), lambda b, r: (b * qb + r, COL_Q)),
            pl.BlockSpec((seq, BRANCH), lambda b, r: (b, COL_K)),
            pl.BlockSpec((seq, BRANCH), lambda b, r: (b, COL_V)),
            pl.BlockSpec((n_ctx, BRANCH), lambda b, r: (b, COL_K)),
            pl.BlockSpec((n_ctx, BRANCH), lambda b, r: (b, COL_V)),
            _layer_spec(bias_tbls.shape, layer),
        ],
        out_specs=pl.BlockSpec((tq, BRANCH), lambda b, r: (b * qb + r, 0)),
        out_shape=jax.ShapeDtypeStruct((batch * seq, BRANCH), BF16),
        scratch_shapes=[pltpu.VMEM((3, NA_HEADS // 2, 2 * tq, NA_KEY_ROWS * GRID_W), F32)],
        compiler_params=_cparams(("arbitrary", "arbitrary")),
        name="na_attn",
    )(proj, proj, proj, projc, projc, bias_tbls)


def _ctx_attn_kernel(q_ref, k_ref, v_ref, o_ref):
    n = q_ref.shape[0]
    first_half = _head_mask(n)
    scale = NA_HEAD_DIM ** -0.5
    nt = (((1,), (1,)), ((), ()))
    for hp in range(NA_HEADS // 2):
        cs = slice(hp * V7X_LANES, (hp + 1) * V7X_LANES)
        qp = q_ref[:, cs] * jnp.asarray(scale, BF16)
        kp = k_ref[:, cs]
        vp = v_ref[:, cs]
        zero = jnp.zeros_like(qp)
        q2 = jnp.concatenate([jnp.where(first_half, qp, zero), jnp.where(first_half, zero, qp)], axis=0)
        s = lax.dot_general(q2, kp, nt, preferred_element_type=F32)
        m = jnp.max(s, axis=-1, keepdims=True)
        p = jnp.exp(s - m)
        denom = jnp.sum(p, axis=-1, keepdims=True)
        o = jnp.dot(p.astype(BF16), vp, preferred_element_type=F32) / denom
        o_ref[:, cs] = jnp.where(first_half, o[0:n], o[n:]).astype(o_ref.dtype)


def _ctx_attn_call(projc, batch, n_ctx):
    return pl.pallas_call(
        _ctx_attn_kernel,
        grid=(batch,),
        in_specs=[
            pl.BlockSpec((n_ctx, BRANCH), lambda b: (b, COL_Q)),
            pl.BlockSpec((n_ctx, BRANCH), lambda b: (b, COL_K)),
            pl.BlockSpec((n_ctx, BRANCH), lambda b: (b, COL_V)),
        ],
        out_specs=pl.BlockSpec((n_ctx, BRANCH), lambda b: (b, 0)),
        out_shape=jax.ShapeDtypeStruct((batch * n_ctx, BRANCH), BF16),
        compiler_params=_cparams(("arbitrary",)),
        name="ctx_attn",
    )(projc, projc, projc)


def _na_bias_table(rpb):
    col = np.arange(GRID_W)
    col_start = np.clip(col - NA_WIN_COLS // 2, 0, GRID_W - NA_WIN_COLS)
    in_win = (col[None, :] >= col_start[:, None]) & (col[None, :] < col_start[:, None] + NA_WIN_COLS)
    dcol = np.clip(col[None, :] - col[:, None] + (NA_WIN_COLS - 1), 0, 2 * NA_WIN_COLS - 2)
    onehot = (dcol[..., None] == np.arange(2 * NA_WIN_COLS - 1)).astype(np.float32)
    t = jnp.einsum('hdc,qwc->hqdw', rpb.astype(F32), onehot, precision=lax.Precision.HIGHEST)
    t = jnp.where(in_win[None, :, None, :], t * LOG2_E, NEG_INF)
    t = t.transpose(0, 2, 1, 3)
    return jnp.concatenate([t, t], axis=-1)


def _na_band_rows(cls, i):
    lo = (0, i, NA_KEY_ROWS - NA_WIN_ROWS)[cls]
    off = (NA_WIN_ROWS - 1 - i, NA_WIN_ROWS // 2 - 1 - i, NA_WIN_ROWS // 2 - 1 - lo - i)[cls]
    return lo, off


def _na_expand_bias(t_ref, bias_ref):
    lane = lax.broadcasted_iota(jnp.int32, (GRID_W, V7X_LANES), 1)
    left = lane < GRID_W
    neg = jnp.full((GRID_W, V7X_LANES), NEG_INF, F32)
    for cls in range(3):
        for hp in range(NA_HEADS // 2):
            for e in range(2):
                for i in range(NA_GROUP_ROWS):
                    lo, off = _na_band_rows(cls, i)
                    r0 = (e * NA_GROUP_ROWS + i) * GRID_W
                    for kt in range(NA_KEY_ROWS // 2):
                        halves = []
                        for kw in (2 * kt, 2 * kt + 1):
                            inside = lo <= kw < lo + NA_WIN_ROWS
                            halves.append(t_ref[2 * hp + e, kw + off] if inside else neg)
                        bias_ref[cls, hp, r0:r0 + GRID_W, kt * V7X_LANES:(kt + 1) * V7X_LANES] = (
                            jnp.where(left, halves[0], halves[1]))


def _cmul(ar, ai, br, bi):
    return ar * br - ai * bi, ar * bi + ai * br


def _ssm_weights(a_re, a_im, log_dt, b_re, b_im, c_re, c_im):
    t_len = SSM_CHUNK
    p, hg = SSM_STATE, SSM_GROUP_DIM
    lg = SSM_LANE_GROUPS
    kf, ef, mf, dec = [], [], [], []
    for d in range(2):
        ar, ai = a_re[d].astype(F32), a_im[d].astype(F32)
        dt = jnp.exp(log_dt[d].astype(F32))[:, None]
        mag = jnp.exp(ar * dt)
        abr, abi = mag * jnp.cos(ai * dt), mag * jnp.sin(ai * dt)
        den = ar * ar + ai * ai
        nr, ni = abr - 1.0, abi
        fr = (nr * ar + ni * ai) / den
        fi = (ni * ar - nr * ai) / den
        bbr, bbi = _cmul(fr[..., None], fi[..., None], b_re[d].astype(F32), b_im[d].astype(F32))
        pws = [(jnp.ones_like(abr), jnp.zeros_like(abr))]
        for _ in range(t_len):
            pws.append(_cmul(pws[-1][0], pws[-1][1], abr, abi))
        pr = jnp.stack([x[0] for x in pws])
        pi = jnp.stack([x[1] for x in pws])
        cr, ci = c_re[d].astype(F32), c_im[d].astype(F32)
        mr, mi = _cmul(cr[None], ci[None], pr[:t_len, :, None, :], pi[:t_len, :, None, :])
        kf.append(jnp.einsum('tghp,gpk->tghk', mr, bbr) - jnp.einsum('tghp,gpk->tghk', mi, bbi))
        e_pw = [pws[t_len - 1 - t] if d == 0 else pws[t] for t in range(t_len)]
        per, pei = jnp.stack([x[0] for x in e_pw]), jnp.stack([x[1] for x in e_pw])
        er, ei = _cmul(per[..., None], pei[..., None], bbr[None], bbi[None])
        ef.append((er, ei))
        c_pw = [pws[t + 1] if d == 0 else pws[t_len - t] for t in range(t_len)]
        pcr, pci = jnp.stack([x[0] for x in c_pw]), jnp.stack([x[1] for x in c_pw])
        m2r, m2i = _cmul(cr[None], ci[None], pcr[:, :, None, :], pci[:, :, None, :])
        mf.append((m2r, m2i))
        dec.append((pr[t_len], pi[t_len]))
    kc = [k.reshape(t_len, SSM_TILES, lg, hg, hg).transpose(0, 1, 4, 2, 3).reshape(
        t_len, SSM_TILES, hg, V7X_LANES) for k in kf]
    z = ([kc[1][t_len - 1 - i] for i in range(t_len - 1)] + [kc[0][0] + kc[1][0]]
         + [kc[0][i] for i in range(1, t_len)])
    ktoe = jnp.stack([jnp.stack(z[t_len - 1 - tp:2 * t_len - 1 - tp]) for tp in range(t_len)])
    parts = [ktoe.transpose(2, 0, 3, 1, 4).reshape(SSM_TILES, t_len, hg, t_len * V7X_LANES)]
    for d in range(2):
        for comp in ef[d]:
            parts.append(comp.reshape(t_len, SSM_TILES, lg, p, hg).transpose(1, 0, 4, 2, 3).reshape(
                SSM_TILES, t_len, hg, lg * p))
    c1 = jnp.concatenate(parts, axis=-1)
    rows = []
    for d in range(2):
        m2r, m2i = mf[d]
        for comp in (m2r, -m2i):
            rows.append(comp.reshape(t_len, SSM_TILES, lg, hg, p).transpose(1, 4, 0, 2, 3).reshape(
                SSM_TILES, p, t_len * V7X_LANES))
    cc = jnp.stack(rows, axis=1)
    decay = jnp.stack([x.reshape(SSM_TILES, SSM_TILES, V7X_LANES) for d in range(2) for x in dec[d]], axis=1)
    return c1.astype(BF16), cc.astype(BF16), decay


def _ssm_kernel(uc_ref, u_ref, c1_ref, cc_ref, dec_ref, dsk_ref, yc_ref, y_ref,
                w1_ref, wc_ref, uf_ref, lhs_ref, yi_ref, es_ref, *, n_ctx, seq, nb):
    t_len = SSM_CHUNK
    nc, lc = n_ctx // t_len, seq // t_len
    nj = nc + lc
    rows_b = n_ctx + seq
    kdim = t_len * V7X_LANES
    sw = SSM_LANE_GROUPS * SSM_STATE
    ntile = sw // V7X_LANES
    srows = nb * ntile
    hg = SSM_GROUP_DIM

    @pl.when(pl.program_id(1) == 0)
    def _():
        lane1 = lax.broadcasted_iota(jnp.int32, (1, kdim + 4 * sw), 1)
        grp1 = jnp.where(lane1 < kdim, (lane1 % V7X_LANES) // hg, ((lane1 - kdim) % sw) // SSM_STATE)
        lanec = lax.broadcasted_iota(jnp.int32, (1, kdim), 1)
        grpc = (lanec % V7X_LANES) // hg
        for t in range(t_len):
            row = c1_ref[0, t]
            for g in range(SSM_LANE_GROUPS):
                r0 = (t * SSM_LANE_GROUPS + g) * hg
                w1_ref[r0:r0 + hg, :] = jnp.where(grp1 == g, row, jnp.zeros_like(row))
        for part in range(4):
            blk = cc_ref[0, part]
            for g in range(SSM_LANE_GROUPS):
                r0 = part * sw + g * SSM_STATE
                wc_ref[r0:r0 + SSM_STATE, :] = jnp.where(grpc == g, blk, jnp.zeros_like(blk))

    for bl in range(nb):
        uf_ref[bl * rows_b:bl * rows_b + n_ctx, :] = uc_ref[bl * n_ctx:(bl + 1) * n_ctx, :].astype(F32)
        uf_ref[bl * rows_b + n_ctx:(bl + 1) * rows_b, :] = u_ref[bl * seq:(bl + 1) * seq, :].astype(F32)
    for t in range(t_len):
        lhs_ref[:, t * V7X_LANES:(t + 1) * V7X_LANES] = uf_ref[pl.ds(t, nb * nj, stride=t_len), :]
    lhs = lhs_ref[...].astype(BF16)
    yi_ref[...] = jnp.dot(lhs, w1_ref[:, 0:kdim], preferred_element_type=F32) + lhs_ref[...] * dsk_ref[0]
    for c in range(4):
        e = jnp.dot(lhs, w1_ref[:, kdim + c * sw:kdim + (c + 1) * sw], preferred_element_type=F32)
        for bl in range(nb):
            for k in range(ntile):
                es_ref[c, pl.ds(bl * ntile + k, nj, stride=srows), :] = (
                    e[bl * nj:(bl + 1) * nj, k * V7X_LANES:(k + 1) * V7X_LANES])

    afr, afi, abr, abi = dec_ref[0, 0], dec_ref[0, 1], dec_ref[0, 2], dec_ref[0, 3]

    def step(k, st):
        sfr, sfi, sbr, sbi = st
        jb = jnp.where(k < nc, nc - 1 - k, nj - 1 - (k - nc))
        rf = pl.ds(pl.multiple_of(k * srows, srows), srows)
        rb = pl.ds(pl.multiple_of(jb * srows, srows), srows)
        efr, efi = es_ref[0, rf, :], es_ref[1, rf, :]
        ebr, ebi = es_ref[2, rb, :], es_ref[3, rb, :]
        es_ref[0, rf, :] = sfr
        es_ref[1, rf, :] = sfi
        es_ref[2, rb, :] = sbr
        es_ref[3, rb, :] = sbi
        nfr, nfi = _cmul(afr, afi, sfr, sfi)
        nbr, nbi = _cmul(abr, abi, sbr, sbi)
        return nfr + efr, nfi + efi, nbr + ebr, nbi + ebi

    z = jnp.zeros((srows, V7X_LANES), F32)
    lax.fori_loop(0, nj, step, (z, z, z, z))

    carried = jnp.concatenate([
        jnp.concatenate([es_ref[c, pl.ds(bl * ntile + k, nj, stride=srows), :]
                         for c in range(4) for k in range(ntile)], axis=1)
        for bl in range(nb)], axis=0)
    yi_ref[...] += jnp.dot(carried.astype(BF16), wc_ref[...], preferred_element_type=F32)
    for t in range(t_len):
        uf_ref[pl.ds(t, nb * nj, stride=t_len), :] = yi_ref[:, t * V7X_LANES:(t + 1) * V7X_LANES]
    for bl in range(nb):
        yc_ref[bl * n_ctx:(bl + 1) * n_ctx, :] = uf_ref[bl * rows_b:bl * rows_b + n_ctx, :]
        y_ref[bl * seq:(bl + 1) * seq, :] = uf_ref[bl * rows_b + n_ctx:(bl + 1) * rows_b, :]


def _ssm_call(proj, projc, c1, cc, decay, dsk_tiled, layer, batch, seq, n_ctx, ctx_col):
    t_len = SSM_CHUNK
    nj = (n_ctx + seq) // t_len
    kdim = t_len * V7X_LANES
    sw4 = 4 * SSM_LANE_GROUPS * SSM_STATE
    ntile = sw4 // 4 // V7X_LANES
    col0 = COL_SSM_U * BRANCH // V7X_LANES
    colc = ctx_col // V7X_LANES
    nb = SSM_BATCH_PER_STEP if batch % SSM_BATCH_PER_STEP == 0 else 1
    assert decay.shape[-2] == nb * ntile
    return pl.pallas_call(
        functools.partial(_ssm_kernel, n_ctx=n_ctx, seq=seq, nb=nb),
        grid=(SSM_TILES, batch // nb),
        in_specs=[
            pl.BlockSpec((nb * n_ctx, V7X_LANES), lambda s, b: (b, colc + s)),
            pl.BlockSpec((nb * seq, V7X_LANES), lambda s, b: (b, col0 + s)),
            pl.BlockSpec((None, 1, t_len, SSM_GROUP_DIM, kdim + sw4), lambda s, b: (layer, s, 0, 0, 0)),
            pl.BlockSpec((None, 1, 4, SSM_STATE, kdim), lambda s, b: (layer, s, 0, 0, 0)),
            pl.BlockSpec((None, 1, 4, nb * ntile, V7X_LANES), lambda s, b: (layer, s, 0, 0, 0)),
            pl.BlockSpec((None, 1, 1, kdim), lambda s, b: (layer, s, 0, 0)),
        ],
        out_specs=[
            pl.BlockSpec((nb * n_ctx, V7X_LANES), lambda s, b: (b, s)),
            pl.BlockSpec((nb * seq, V7X_LANES), lambda s, b: (b, s)),
        ],
        out_shape=[
            jax.ShapeDtypeStruct((batch * n_ctx, BRANCH), F32),
            jax.ShapeDtypeStruct((batch * seq, BRANCH), F32),
        ],
        scratch_shapes=[
            pltpu.VMEM((kdim, kdim + sw4), BF16),
            pltpu.VMEM((sw4, kdim), BF16),
            pltpu.VMEM((nb * (n_ctx + seq), V7X_LANES), F32),
            pltpu.VMEM((nb * nj, kdim), F32),
            pltpu.VMEM((nb * nj, kdim), F32),
            pltpu.VMEM((4, nb * nj * ntile, V7X_LANES), F32),
        ],
        compiler_params=_cparams(("arbitrary", "arbitrary")),
        name="ssm",
    )(projc, proj, c1, cc, decay, dsk_tiled)


def _merge_kernel(x_ref, gate_ref, gpost_ref,
                  naz_ref, pu_ref, pz_ref, cx_ref, cb_ref, cc_ref, cz_ref, sz_ref,
                  lg0_ref, lg1_ref, lg2_ref, lg3_ref,
                  pu_prev_ref, pu_next_ref, cx_prev_ref, cx_next_ref, cc_prev_ref, cc_next_ref,
                  ona_ref, yssm_ref,
                  poolw_ref, pscale_ref, convw_ref, gluw_ref, bgate_ref, wbr_ref, wo_ref,
                  o_ref, pad_ref, s2_ref, s4_ref, *, tm, rows_per_seq):
    i = pl.program_id(0)
    tiles_per_seq = rows_per_seq // tm
    ti = i % tiles_per_seq
    has_prev = ti > 0
    has_next = ti < tiles_per_seq - 1
    t0 = ti * tm
    pos = t0 + lax.broadcasted_iota(jnp.int32, (tm, 1), 0)

    pad_ref[0:HALO, :] = jnp.where(has_prev, pu_prev_ref[...].astype(F32), 0.0)
    pad_ref[HALO:HALO + tm, :] = pu_ref[...].astype(F32)
    pad_ref[HALO + tm:2 * HALO + tm, :] = jnp.where(has_next, pu_next_ref[...].astype(F32), 0.0)
    base = HALO - max(POOL_WINDOWS) // 2
    n8, n4, n2 = tm + 16, tm + 24, tm + 32
    mixed = []
    for gi, w in enumerate(POOL_WINDOWS):
        cs = slice(gi * POOL_GROUP_DIM, (gi + 1) * POOL_GROUP_DIM)
        half = pad_ref
        if w >= 4:
            s2_ref[base:base + n2, cs] = pad_ref[base:base + n2, cs] + pad_ref[base + 1:base + 1 + n2, cs]
            half = s2_ref
        if w >= 8:
            s4_ref[base:base + n4, cs] = s2_ref[base:base + n4, cs] + s2_ref[base + 2:base + 2 + n4, cs]
            half = s4_ref
        if w >= 16:
            s2_ref[base:base + n8, cs] = s4_ref[base:base + n8, cs] + s4_ref[base + 4:base + 4 + n8, cs]
            half = s2_ref
        acc = half[HALO - w // 2:HALO - w // 2 + tm, cs] + half[HALO:HALO + tm, cs]
        lo = jnp.maximum(pos - w // 2, 0)
        hi = jnp.minimum(pos + w - w // 2, rows_per_seq)
        cnt = (hi - lo).astype(F32)
        pooled = acc / cnt - pad_ref[HALO:HALO + tm, cs]
        mixed.append(jnp.dot(pooled.astype(BF16), poolw_ref[gi], preferred_element_type=F32))
    o_pool = jnp.concatenate(mixed, axis=-1) * pscale_ref[...]

    pad_ref[0:HALO, :] = jnp.where(
        has_prev, cc_prev_ref[...].astype(F32) * cx_prev_ref[...].astype(F32), 0.0)
    pad_ref[HALO:HALO + tm, :] = cc_ref[...].astype(F32) * cx_ref[...].astype(F32)
    pad_ref[HALO + tm:2 * HALO + tm, :] = jnp.where(
        has_next, cc_next_ref[...].astype(F32) * cx_next_ref[...].astype(F32), 0.0)
    cw = convw_ref[...]
    conv = (pad_ref[HALO - 1:HALO - 1 + tm, :] * cw[0:1] + pad_ref[HALO:HALO + tm, :] * cw[1:2]
            + pad_ref[HALO + 1:HALO + 1 + tm, :] * cw[2:3])
    o_conv = cb_ref[...].astype(F32) * conv

    g = jax.nn.gelu(yssm_ref[...]).astype(BF16)
    gg = jnp.dot(g, gluw_ref[...], preferred_element_type=F32)
    ga = 0.5 * gg[:, 0:BRANCH]
    o_ssm = ga + ga * jnp.tanh(0.5 * gg[:, BRANCH:2 * BRANCH])

    outs = (ona_ref[...].astype(F32), o_pool, o_conv, o_ssm)
    zs = (naz_ref, pz_ref, cz_ref, sz_ref)
    lgs = (lg0_ref, lg1_ref, lg2_ref, lg3_ref)
    merged = None
    for bi in range(N_BRANCHES):
        hz = 0.5 * zs[bi][...].astype(F32)
        a = (outs[bi] * (hz + hz * jnp.tanh(hz))).astype(BF16)
        br = jnp.dot(a, wbr_ref[bi * BRANCH:(bi + 1) * BRANCH, :], preferred_element_type=F32)
        th = jnp.tanh(lgs[bi][...].astype(F32) + bgate_ref[:, bi * D_MODEL:(bi + 1) * D_MODEL])
        term = (1.0 + th) * br
        merged = term if merged is None else merged + term
    y = jnp.dot(merged.astype(BF16), wo_ref[...], preferred_element_type=F32)
    ms = jnp.mean(y * y, axis=-1, keepdims=True)
    gain = gpost_ref[...] * gate_ref[0]
    o_ref[...] = x_ref[...] + y * lax.rsqrt(ms + RMS_EPS) * gain


def _merge_call(x2, gate, g_post, proj, o_na, y_ssm, pool_w, pool_scale, conv_w, glu_w, b_gate,
                w_br, w_o, layer, rows_per_seq, tm):
    r, d = x2.shape
    tiles_per_seq = rows_per_seq // tm
    hb = tm // HALO
    n_halo_blocks = r // HALO

    def col(c):
        return pl.BlockSpec((tm, BRANCH), lambda i: (i, c))

    def lg(c):
        return pl.BlockSpec((tm, D_MODEL), lambda i: (i, COL_MERGE * BRANCH // D_MODEL + c))

    def prev(c):
        return pl.BlockSpec((HALO, BRANCH), lambda i: (jnp.maximum(i * hb - 1, 0), c))

    def nxt(c):
        return pl.BlockSpec((HALO, BRANCH), lambda i: (jnp.minimum((i + 1) * hb, n_halo_blocks - 1), c))

    in_specs = [
        pl.BlockSpec((tm, d), lambda i: (i, 0)),
        pl.BlockSpec((1, 1, d), lambda i: (i // tiles_per_seq, 0, 0)),
        pl.BlockSpec((1, d), lambda i: (0, 0)),
        col(COL_NA_Z), col(COL_POOL_U), col(COL_POOL_Z), col(COL_CONV_X), col(COL_CONV_B),
        col(COL_CONV_C), col(COL_CONV_Z), col(COL_SSM_Z),
        lg(0), lg(1), lg(2), lg(3),
        prev(COL_POOL_U), nxt(COL_POOL_U), prev(COL_CONV_X), nxt(COL_CONV_X),
        prev(COL_CONV_C), nxt(COL_CONV_C),
        pl.BlockSpec((tm, BRANCH), lambda i: (i, 0)),
        pl.BlockSpec((tm, BRANCH), lambda i: (i, 0)),
        _layer_spec(pool_w.shape, layer), _const_spec(pool_scale.shape), _const_spec(conv_w.shape),
        _layer_spec(glu_w.shape, layer), _const_spec(b_gate.shape), _layer_spec(w_br.shape, layer),
        _layer_spec(w_o.shape, layer),
    ]
    args = [x2, gate, g_post] + [proj] * 8 + [proj] * 4 + [proj] * 6 + [
        o_na, y_ssm, pool_w, pool_scale, conv_w, glu_w, b_gate, w_br, w_o]
    return pl.pallas_call(
        functools.partial(_merge_kernel, tm=tm, rows_per_seq=rows_per_seq),
        grid=(r // tm,),
        in_specs=in_specs,
        out_specs=pl.BlockSpec((tm, d), lambda i: (i, 0)),
        out_shape=jax.ShapeDtypeStruct((r, d), F32),
        scratch_shapes=[pltpu.VMEM((tm + 2 * HALO, BRANCH), F32)] * 3,
        compiler_params=_cparams(("arbitrary",)),
        name="merge",
    )(*args)


def _tile_rows(rows, pref):
    t = min(rows, pref)
    assert rows % t == 0
    return t


def kernel(x, c, ctx, c_ctx, w_mod, b_mod, g_pre, g_post, w_in, b_gate, na_rpb, pool_w, pool_scale,
           conv_w, ssm_a_re, ssm_a_im, ssm_log_dt, ssm_b_re, ssm_b_im, ssm_c_re, ssm_c_im, ssm_d,
           glu_w, w_br, w_o):
    batch, seq, d = x.shape
    n_ctx = ctx.shape[1]
    depth = w_mod.shape[0]
    assert d == D_MODEL and seq % (GRID_W * NA_WIN_ROWS) == 0 and batch + 1 <= 8
    assert n_ctx % HALO == 0 and w_in.shape[-1] == IN_TOTAL

    c8 = jnp.concatenate([c, c_ctx[None], jnp.zeros((7 - batch, d), F32)], axis=0)
    mod = _mod_call(c8, w_mod, b_mod)

    x2 = x.reshape(batch * seq, d)
    xc2 = ctx.reshape(batch * n_ctx, d)
    tm_in = _tile_rows(seq, INPROJ_ROW_TILE)
    tm_in_c = _tile_rows(batch * n_ctx, INPROJ_ROW_TILE)
    tm_mg = _tile_rows(seq, MERGE_ROW_TILE)
    tm_mg_c = _tile_rows(n_ctx, MERGE_ROW_TILE)
    tn_in = INPROJ_COL_TILE

    bias_tbls = jax.vmap(_na_bias_table)(na_rpb)
    c1s, ccs, decays = jax.vmap(_ssm_weights)(ssm_a_re, ssm_a_im, ssm_log_dt, ssm_b_re, ssm_b_im,
                                              ssm_c_re, ssm_c_im)
    if batch % SSM_BATCH_PER_STEP == 0:
        decays = jnp.tile(decays, (1, 1, 1, SSM_BATCH_PER_STEP, 1))
    dsks = jnp.tile(ssm_d.astype(F32).reshape(depth, SSM_TILES, 1, V7X_LANES), (1, 1, 1, SSM_CHUNK))
    col_scale = jnp.where(jnp.arange(IN_TOTAL) < COL_MERGE * BRANCH, 1.0, 0.5).astype(F32)
    w_in_bf = (w_in * col_scale).astype(BF16)
    pool_w_bf, glu_w_bf = pool_w.astype(BF16), glu_w.astype(BF16)
    w_br_bf, w_o_bf = w_br.astype(BF16), (0.5 * w_o).astype(BF16)

    for i in range(depth):
        with_ctx_out = i < depth - 1
        shift, scale, gate = (mod[i, :, k * d:(k + 1) * d] for k in range(3))
        lat = lambda v: v[:batch].reshape(batch, 1, d)
        cx = lambda v: v[batch:batch + 1].reshape(1, 1, d)
        gp = g_pre[i].reshape(1, d)
        gq = g_post[i].reshape(1, d)

        proj = _inproj_call(x2, lat(scale), lat(shift), gp, w_in_bf, i, seq, tm_in, tn_in)
        if with_ctx_out:
            projc = _inproj_call(xc2, cx(scale), cx(shift), gp, w_in_bf, i, batch * n_ctx, tm_in_c, tn_in)
            ctx_u_col = COL_SSM_U * BRANCH
        else:
            assert COL_V * BRANCH < tn_in and (COL_SSM_U * BRANCH) // tn_in == 2
            projc = _inproj_call(xc2, cx(scale), cx(shift), gp, w_in_bf, i, batch * n_ctx, tm_in_c, tn_in,
                                 col_tiles=(2, 2))
            ctx_u_col = COL_SSM_U * BRANCH - tn_in

        o_na = _na_call(proj, projc, bias_tbls, i, batch, seq, n_ctx)
        yc_ssm, y_ssm = _ssm_call(proj, projc, c1s, ccs, decays, dsks, i, batch, seq, n_ctx, ctx_u_col)

        mw = (pool_w_bf, pool_scale[i].reshape(1, BRANCH), conv_w[i], glu_w_bf,
              0.5 * b_gate[i].reshape(1, N_BRANCHES * d), w_br_bf, w_o_bf, i)
        x2 = _merge_call(x2, lat(gate), gq, proj, o_na, y_ssm, *mw, seq, tm_mg)
        if with_ctx_out:
            oc_na = _ctx_attn_call(projc, batch, n_ctx)
            gate_c = jnp.broadcast_to(cx(gate), (batch, 1, d))
            xc2 = _merge_call(xc2, gate_c, gq, projc, oc_na, yc_ssm, *mw, n_ctx, tm_mg_c)
    return x2.reshape(batch, seq, d)
```
